```python
import jax, jax.numpy as jnp
from jax import lax
import numpy as np

D_MODEL = 1024
BATCH = 8
SEQ = 2048
DEPTH = 4
DEC_BATCH = 128
DEC_SEQ = 1
PAST_LEN = 16384
PAGE_SIZE = 128

N_EVEN = (DEPTH + 1) // 2
N_ODD = DEPTH // 2
D_MIX = D_MODEL
D_HALF = D_MIX // 2
H_A = 4
DV_A = D_HALF // H_A
DK_A = DV_A // 2
GLA_RANK = 16
GLA_TAU = 16.0
H_B = 4
DV_B = D_HALF // H_B
DK_B = DV_B // 2
ROPE_BASE = 10000.0
H_C = 4
DK_C = D_HALF // H_C
DV_C = D_HALF // H_C
D_RG = D_HALF
RG_BLOCKS = 8
RG_BW = D_RG // RG_BLOCKS
RG_C = 8.0
CONV_W = 4
D_FF = (8 * D_MODEL + 3 * 256 - 1) // (3 * 256) * 256
CHUNK = 64
EPS = 1e-6

EVEN_SIZES = (H_A * DK_A, H_A * DK_A, H_A * DV_A, D_HALF, GLA_RANK,
              H_B * DK_B, H_B * DK_B, H_B * DV_B, D_HALF)
ODD_SIZES = (H_C * DK_C, H_C * DK_C, H_C * DV_C, H_C, H_C, H_C * DV_C, D_RG, D_RG)
D_IN_EVEN = sum(EVEN_SIZES)
D_IN_ODD = sum(ODD_SIZES)

kernel_name = "hybrid_gla_retnet_mlstm_rglru_step"


def _split_points(sizes):
    return [int(s) for s in np.cumsum(sizes)[:-1]]


def rmsnorm(x, g):
    xf = x.astype(jnp.float32)
    y = xf * lax.rsqrt(jnp.mean(xf * xf, axis=-1, keepdims=True) + EPS)
    return (y * g).astype(x.dtype)


def head_rmsnorm(o, g):
    B, L, H, d = o.shape
    of = o.astype(jnp.float32)
    y = of * lax.rsqrt(jnp.mean(of * of, axis=-1, keepdims=True) + EPS)
    return y.reshape(B, L, H * d) * g


def head_groupnorm(o, g):
    B, L, H, d = o.shape
    of = o.astype(jnp.float32)
    mu = jnp.mean(of, axis=-1, keepdims=True)
    c = of - mu
    y = c * lax.rsqrt(jnp.mean(c * c, axis=-1, keepdims=True) + EPS)
    return y.reshape(B, L, H * d) * g


def rope(x, pos):
    half = x.shape[-1] // 2
    inv = ROPE_BASE ** (-jnp.arange(half, dtype=jnp.float32) / half)
    ang = pos.astype(jnp.float32)[:, None] * inv[None, :]
    cos = jnp.cos(ang)[None, :, None, :]
    sin = jnp.sin(ang)[None, :, None, :]
    x1, x2 = x[..., :half], x[..., half:]
    return jnp.concatenate([x1 * cos - x2 * sin, x1 * sin + x2 * cos], axis=-1)


def chunk_len(L):
    return CHUNK if L % CHUNK == 0 else L


def to_chunks(t, c):
    B, L = t.shape[:2]
    return jnp.moveaxis(t.reshape(B, L // c, c, *t.shape[2:]), 1, 0)


def from_chunks(t):
    t = jnp.moveaxis(t, 0, 1)
    return t.reshape(t.shape[0], -1, *t.shape[3:])


def gla_scan(q, k, v, logf, S0):
    c = chunk_len(q.shape[1])
    causal = jnp.tril(jnp.ones((c, c), dtype=bool))

    def step(S, inp):
        qc, kc, vc, gc = inp
        b = jnp.cumsum(gc, axis=1)
        o_inter = jnp.einsum('bthk,bhkv->bthv', qc * jnp.exp(b), S)
        diff = b[:, :, None] - b[:, None, :]
        decay = jnp.exp(jnp.where(causal[None, :, :, None, None], diff, -jnp.inf))
        scores = jnp.einsum('bthk,bshk,btshk->bhts', qc, kc, decay)
        o_intra = jnp.einsum('bhts,bshv->bthv', scores, vc)
        b_last = b[:, -1]
        k_dec = kc * jnp.exp(b_last[:, None] - b)
        S_new = S * jnp.exp(b_last)[..., None] + jnp.einsum('bshk,bshv->bhkv', k_dec, vc)
        return S_new, o_inter + o_intra

    S, o = lax.scan(step, S0.astype(jnp.float32),
                    (to_chunks(q, c), to_chunks(k, c), to_chunks(v, c), to_chunks(logf, c)))
    return from_chunks(o), S.astype(S0.dtype)


def retention_scan(q, k, v, log_gamma, S0):
    c = chunk_len(q.shape[1])
    t = jnp.arange(c, dtype=jnp.float32)
    diff = t[:, None] - t[None, :]
    decay = jnp.where(diff[None] >= 0,
                      jnp.exp(jnp.maximum(diff, 0.0)[None] * log_gamma[:, None, None]), 0.0)
    q_dec = jnp.exp((t + 1.0)[:, None] * log_gamma[None, :])
    k_dec = jnp.exp((c - 1.0 - t)[:, None] * log_gamma[None, :])
    g_chunk = jnp.exp(c * log_gamma)

    def step(S, inp):
        qc, kc, vc = inp
        o_inter = jnp.einsum('bthk,bhkv->bthv', qc * q_dec[None, :, :, None], S)
        scores = jnp.einsum('bthk,bshk->bhts', qc, kc) * decay[None]
        o_intra = jnp.einsum('bhts,bshv->bthv', scores, vc)
        S_new = S * g_chunk[None, :, None, None] + jnp.einsum(
            'bshk,bshv->bhkv', kc * k_dec[None, :, :, None], vc)
        return S_new, o_inter + o_intra

    S, o = lax.scan(step, S0.astype(jnp.float32), (to_chunks(q, c), to_chunks(k, c), to_chunks(v, c)))
    return from_chunks(o), S.astype(S0.dtype)


def mlstm_scan(q, k, v, ig, logf, C0, n0, m0):
    c = chunk_len(q.shape[1])
    causal = jnp.tril(jnp.ones((c, c), dtype=bool))

    def step(carry, inp):
        C, n, m = carry
        qc, kc, vc, ic, fc = inp
        b = jnp.cumsum(fc, axis=1)
        a = ic - b
        m_t = jnp.maximum(b + m[:, None], b + lax.cummax(a, axis=1))
        w_state = jnp.exp(b + m[:, None] - m_t)
        logw = b[:, :, None] + a[:, None, :] - m_t[:, :, None]
        w_intra = jnp.exp(jnp.where(causal[None, :, :, None], logw, -jnp.inf))
        scores = jnp.einsum('bthk,bshk->btsh', qc, kc) * w_intra
        num = w_state[..., None] * jnp.einsum('bthk,bhkv->bthv', qc, C) + \
            jnp.einsum('btsh,bshv->bthv', scores, vc)
        den = w_state * jnp.einsum('bthk,bhk->bth', qc, n) + jnp.sum(scores, axis=2)
        h = num / jnp.maximum(jnp.abs(den), jnp.exp(-m_t))[..., None]
        m_last = m_t[:, -1]
        w_s_last = jnp.exp(b[:, -1] + m - m_last)
        w_k = jnp.exp(b[:, -1:] + a - m_last[:, None])
        kw = kc * w_k[..., None]
        C_new = C * w_s_last[..., None, None] + jnp.einsum('bshk,bshv->bhkv', kw, vc)
        n_new = n * w_s_last[..., None] + jnp.sum(kw, axis=1)
        return (C_new, n_new, m_last), h

    (C, n, m), h = lax.scan(
        step, (C0.astype(jnp.float32), n0.astype(jnp.float32), m0.astype(jnp.float32)),
        (to_chunks(q, c), to_chunks(k, c), to_chunks(v, c), to_chunks(ig, c), to_chunks(logf, c)))
    return from_chunks(h), C.astype(C0.dtype), n.astype(n0.dtype), m.astype(m0.dtype)


def causal_conv(x, buf, w, bias):
    L = x.shape[1]
    xp = jnp.concatenate([buf.astype(x.dtype), x], axis=1)
    y = bias + sum(xp[:, j:j + L] * w[j] for j in range(CONV_W))
    return y, xp[:, -(CONV_W - 1):]


def rglru(x, h0, w_r, b_r, w_i, b_i, lam):
    B, L, _ = x.shape
    xb = x.reshape(B, L, RG_BLOCKS, RG_BW)
    r = jax.nn.sigmoid(jnp.einsum('blnc,ncd->blnd', xb, w_r).reshape(B, L, D_RG) + b_r)
    i = jax.nn.sigmoid(jnp.einsum('blnc,ncd->blnd', xb, w_i).reshape(B, L, D_RG) + b_i)
    log_a = -RG_C * r * jax.nn.softplus(-lam)
    a = jnp.exp(log_a)
    u = jnp.sqrt(-jnp.expm1(2.0 * log_a)) * (i * x)

    def combine(e1, e2):
        a1, b1 = e1
        a2, b2 = e2
        return a1 * a2, a2 * b1 + b2

    A, Bc = lax.associative_scan(combine, (a, u), axis=1)
    h = A * h0.astype(jnp.float32)[:, None] + Bc
    return h, h[:, -1].astype(h0.dtype)


def retention_log_gamma():
    return jnp.log1p(-jnp.exp2(-5.0 - jnp.arange(H_B, dtype=jnp.float32)))


def mixer_even(h, pos, S_gla, S_ret, w_in, w_lr, b_lr, gn_gla, gn_ret, w_out):
    B, L, _ = h.shape
    proj = (h @ w_in).astype(jnp.float32)
    gq, gk, gv, gg, glr, rq, rk, rv, rg = jnp.split(proj, _split_points(EVEN_SIZES), axis=-1)
    q = gq.reshape(B, L, H_A, DK_A) * DK_A ** -0.5
    k = gk.reshape(B, L, H_A, DK_A)
    v = gv.reshape(B, L, H_A, DV_A)
    logf = (jax.nn.log_sigmoid(glr @ w_lr + b_lr) / GLA_TAU).reshape(B, L, H_A, DK_A)
    o_a, S_gla_new = gla_scan(q, k, v, logf, S_gla)
    o_a = head_rmsnorm(o_a, gn_gla) * jax.nn.silu(gg)
    q = rope(rq.reshape(B, L, H_B, DK_B), pos) * DK_B ** -0.5
    k = rope(rk.reshape(B, L, H_B, DK_B), pos)
    v = rv.reshape(B, L, H_B, DV_B)
    o_b, S_ret_new = retention_scan(q, k, v, retention_log_gamma(), S_ret)
    o_b = head_groupnorm(o_b, gn_ret) * jax.nn.silu(rg)
    out = jnp.concatenate([o_a, o_b], axis=-1).astype(h.dtype) @ w_out
    return out, S_gla_new, S_ret_new


def mixer_odd(h, C, n, m, h_rg, conv_buf, w_in, b_ig, b_fg, gn_ml, conv_w, conv_b,
              w_r, b_r, w_i, b_i, lam, w_out):
    B, L, _ = h.shape
    proj = (h @ w_in).astype(jnp.float32)
    mq, mk, mv, mi, mf, mo, rx, rgate = jnp.split(proj, _split_points(ODD_SIZES), axis=-1)
    q = mq.reshape(B, L, H_C, DK_C) * DK_C ** -0.5
    k = mk.reshape(B, L, H_C, DK_C)
    v = mv.reshape(B, L, H_C, DV_C)
    ig = mi + b_ig
    logf = jax.nn.log_sigmoid(mf + b_fg)
    o_c, C_new, n_new, m_new = mlstm_scan(q, k, v, ig, logf, C, n, m)
    o_c = head_rmsnorm(o_c, gn_ml) * jax.nn.sigmoid(mo)
    xc, conv_new = causal_conv(rx, conv_buf, conv_w, conv_b)
    o_d, h_new = rglru(xc, h_rg, w_r, b_r, w_i, b_i, lam)
    o_d = o_d * jax.nn.gelu(rgate, approximate=True)
    out = jnp.concatenate([o_c, o_d], axis=-1).astype(h.dtype) @ w_out
    return out, C_new, n_new, m_new, h_new, conv_new


def swiglu(h, w_up, w_down):
    g, u = jnp.split(h @ w_up, 2, axis=-1)
    return (jax.nn.silu(g) * u) @ w_down


def run_trunk(x, pos, s_gla, s_ret, s_C, s_n, s_m, s_h, s_conv, w):
    (norm_mix, norm_ffn, norm_final,
     even_w_in, even_w_lr, even_b_lr, even_gn_gla, even_gn_ret, even_w_out,
     odd_w_in, ml_b_i, ml_b_f, odd_gn_ml, rg_conv_w, rg_conv_b, rg_w_r, rg_b_r, rg_w_i, rg_b_i,
     rg_lam, odd_w_out, ffn_w_up, ffn_w_down) = w
    gla_l, ret_l, C_l, n_l, m_l, h_l, conv_l = [], [], [], [], [], [], []
    for layer in range(DEPTH):
        hn = rmsnorm(x, norm_mix[layer])
        if layer % 2 == 0:
            e = layer // 2
            mix, sg, sr = mixer_even(hn, pos, s_gla[e], s_ret[e], even_w_in[e], even_w_lr[e],
                                     even_b_lr[e], even_gn_gla[e], even_gn_ret[e], even_w_out[e])
            gla_l.append(sg)
            ret_l.append(sr)
        else:
            o = layer // 2
            mix, c_, n_, m_, h_, cv = mixer_odd(hn, s_C[o], s_n[o], s_m[o], s_h[o], s_conv[o],
                                                odd_w_in[o], ml_b_i[o], ml_b_f[o], odd_gn_ml[o],
                                                rg_conv_w[o], rg_conv_b[o], rg_w_r[o], rg_b_r[o],
                                                rg_w_i[o], rg_b_i[o], rg_lam[o], odd_w_out[o])
            C_l.append(c_)
            n_l.append(n_)
            m_l.append(m_)
            h_l.append(h_)
            conv_l.append(cv)
        x = x + mix
        x = x + swiglu(rmsnorm(x, norm_ffn[layer]), ffn_w_up[layer], ffn_w_down[layer])
    return (rmsnorm(x, norm_final), jnp.stack(gla_l), jnp.stack(ret_l), jnp.stack(C_l),
            jnp.stack(n_l), jnp.stack(m_l), jnp.stack(h_l), jnp.stack(conv_l))


def setup_inputs(seed: int = 0) -> dict:
    key = jax.random.key(seed)
    ks = iter(jax.random.split(key, 48))

    def nrm(shape, scale):
        return jax.random.normal(next(ks), shape, jnp.float32) * scale

    def gain(shape):
        return 1.0 + 0.05 * jax.random.normal(next(ks), shape, jnp.float32)

    a_rg = jax.random.uniform(next(ks), (N_ODD, D_RG), jnp.float32, minval=0.9, maxval=0.999)
    return {
        "x_prompt": nrm((BATCH, SEQ, D_MODEL), 1.0),
        "x_sample": nrm((DEC_BATCH, DEC_SEQ, D_MODEL), 1.0),
        "state_gla": nrm((N_EVEN, DEC_BATCH, H_A, DK_A, DV_A), 1.0),
        "state_ret": nrm((N_EVEN, DEC_BATCH, H_B, DK_B, DV_B), 1.0),
        "state_mlstm_C": nrm((N_ODD, DEC_BATCH, H_C, DK_C, DV_C), 1.0),
        "state_mlstm_n": nrm((N_ODD, DEC_BATCH, H_C, DK_C), 1.0),
        "state_mlstm_m": nrm((N_ODD, DEC_BATCH, H_C), 2.0),
        "state_rglru_h": nrm((N_ODD, DEC_BATCH, D_RG), 0.5),
        "state_rglru_conv": nrm((N_ODD, DEC_BATCH, CONV_W - 1, D_RG), 1.0),
        "norm_mix": gain((DEPTH, D_MODEL)),
        "norm_ffn": gain((DEPTH, D_MODEL)),
        "norm_final": gain((D_MODEL,)),
        "even_w_in": nrm((N_EVEN, D_MODEL, D_IN_EVEN), D_MODEL ** -0.5),
        "even_w_lr": nrm((N_EVEN, GLA_RANK, H_A * DK_A), GLA_RANK ** -0.5),
        "even_b_lr": nrm((N_EVEN, H_A * DK_A), 0.1),
        "even_gn_gla": gain((N_EVEN, D_HALF)),
        "even_gn_ret": gain((N_EVEN, D_HALF)),
        "even_w_out": nrm((N_EVEN, D_MIX, D_MODEL), D_MIX ** -0.5),
        "odd_w_in": nrm((N_ODD, D_MODEL, D_IN_ODD), D_MODEL ** -0.5),
        "ml_b_i": nrm((N_ODD, H_C), 0.1),
        "ml_b_f": jnp.linspace(3.0, 6.0, H_C, dtype=jnp.float32)[None, :] + nrm((N_ODD, H_C), 0.01),
        "odd_gn_ml": gain((N_ODD, D_HALF)),
        "rg_conv_w": nrm((N_ODD, CONV_W, D_RG), CONV_W ** -0.5),
        "rg_conv_b": nrm((N_ODD, D_RG), 0.02),
        "rg_w_r": nrm((N_ODD, RG_BLOCKS, RG_BW, RG_BW), RG_BW ** -0.5),
        "rg_b_r": nrm((N_ODD, D_RG), 0.1),
        "rg_w_i": nrm((N_ODD, RG_BLOCKS, RG_BW, RG_BW), RG_BW ** -0.5),
        "rg_b_i": nrm((N_ODD, D_RG), 0.1),
        "rg_lam": jnp.log(a_rg) - jnp.log1p(-a_rg),
        "odd_w_out": nrm((N_ODD, D_MIX, D_MODEL), D_MIX ** -0.5),
        "ffn_w_up": nrm((DEPTH, D_MODEL, 2 * D_FF), D_MODEL ** -0.5),
        "ffn_w_down": nrm((DEPTH, D_FF, D_MODEL), D_FF ** -0.5),
    }


def reference(x_prompt, x_sample, state_gla, state_ret, state_mlstm_C, state_mlstm_n,
              state_mlstm_m, state_rglru_h, state_rglru_conv,
              norm_mix, norm_ffn, norm_final,
              even_w_in, even_w_lr, even_b_lr, even_gn_gla, even_gn_ret, even_w_out,
              odd_w_in, ml_b_i, ml_b_f, odd_gn_ml, rg_conv_w, rg_conv_b, rg_w_r, rg_b_r,
              rg_w_i, rg_b_i, rg_lam, odd_w_out, ffn_w_up, ffn_w_down):
    w = (norm_mix, norm_ffn, norm_final,
         even_w_in, even_w_lr, even_b_lr, even_gn_gla, even_gn_ret, even_w_out,
         odd_w_in, ml_b_i, ml_b_f, odd_gn_ml, rg_conv_w, rg_conv_b, rg_w_r, rg_b_r,
         rg_w_i, rg_b_i, rg_lam, odd_w_out, ffn_w_up, ffn_w_down)
    Bp, Lp = x_prompt.shape[0], x_prompt.shape[1]
    dt = state_gla.dtype
    (y_p, gla_p, ret_p, C_p, n_p, m_p, h_p, conv_p) = run_trunk(
        x_prompt, jnp.arange(Lp, dtype=jnp.int32),
        jnp.zeros((N_EVEN, Bp, H_A, DK_A, DV_A), dt),
        jnp.zeros((N_EVEN, Bp, H_B, DK_B, DV_B), dt),
        jnp.zeros((N_ODD, Bp, H_C, DK_C, DV_C), dt),
        jnp.zeros((N_ODD, Bp, H_C, DK_C), dt),
        jnp.zeros((N_ODD, Bp, H_C), dt),
        jnp.zeros((N_ODD, Bp, D_RG), dt),
        jnp.zeros((N_ODD, Bp, CONV_W - 1, D_RG), dt),
        w)
    (y_s, gla_s, ret_s, C_s, n_s, m_s, h_s, conv_s) = run_trunk(
        x_sample, PAST_LEN + jnp.arange(x_sample.shape[1], dtype=jnp.int32),
        state_gla, state_ret, state_mlstm_C, state_mlstm_n, state_mlstm_m,
        state_rglru_h, state_rglru_conv, w)
    return (y_p, y_s, gla_p, ret_p, C_p, n_p, m_p, h_p, conv_p,
            gla_s, ret_s, C_s, n_s, m_s, h_s, conv_s)
```

```python
import functools

import numpy as np
import jax
import jax.numpy as jnp
from jax import lax
from jax.experimental import pallas as pl
from jax.experimental.pallas import tpu as pltpu

F32 = jnp.float32
BF16 = jnp.bfloat16

D_MODEL = 1024
DEPTH = 4
PAST_LEN = 16384
N_EVEN = (DEPTH + 1) // 2
N_ODD = DEPTH // 2
D_HALF = D_MODEL // 2
H_A = 4
DV_A = D_HALF // H_A
DK_A = DV_A // 2
GLA_RANK = 16
GLA_TAU = 16.0
H_B = 4
DV_B = D_HALF // H_B
DK_B = DV_B // 2
ROPE_BASE = 10000.0
H_C = 4
DK_C = D_HALF // H_C
DV_C = D_HALF // H_C
D_RG = D_HALF
RG_BLOCKS = 8
RG_BW = D_RG // RG_BLOCKS
RG_C = 8.0
CONV_W = 4
D_FF = (8 * D_MODEL + 3 * 256 - 1) // (3 * 256) * 256
EPS = 1e-6

LANE = 128
D_PROJ = 3072 + LANE
GATE_OFF = 3072
EVEN_CHUNK = 64
GLA_SUB = 16
ODD_CHUNK = 128
STEP_BB = 8
VMEM_LIMIT = 56 * 1024 * 1024

LOG_GAMMA = [float(np.log1p(-np.exp2(-5.0 - h))) for h in range(H_B)]


def _dot(a, b):
    return jnp.dot(a.astype(BF16), b.astype(BF16), preferred_element_type=F32)


def _dot_nt(a, b):
    return lax.dot_general(a.astype(BF16), b.astype(BF16), (((1,), (1,)), ((), ())),
                           preferred_element_type=F32)


def _dot_f32(a, b):
    return jnp.dot(a, b, preferred_element_type=F32, precision=lax.Precision.HIGHEST)


def _rms(x, g):
    return x * lax.rsqrt(jnp.mean(x * x, axis=-1, keepdims=True) + EPS) * g


def _softplus(x):
    return jnp.maximum(x, 0.0) + jnp.log1p(jnp.exp(-jnp.abs(x)))


def _log_sigmoid(x):
    return -_softplus(-x)


def _sigmoid(x):
    return 1.0 / (1.0 + jnp.exp(-x))


def _silu(x):
    return x * _sigmoid(x)


def _gelu_tanh(x):
    return 0.5 * x * (1.0 + jnp.tanh(0.7978845608028654 * (x + 0.044715 * x * x * x)))


def _iota(shape, dim):
    return lax.broadcasted_iota(jnp.int32, shape, dim)


def _tri(c):
    return (_iota((c, c), 0) >= _iota((c, c), 1)).astype(F32)


def _stack_heads(x, n_heads, dk):
    head = _iota(x.shape, 1) // dk
    return jnp.concatenate([jnp.where(head == h, x, 0.0) for h in range(n_heads)], axis=0)


def _diag_lanes(y, n_heads, rows, cols):
    return jnp.concatenate(
        [y[h * rows:(h + 1) * rows, h * cols:(h + 1) * cols] for h in range(n_heads)], axis=1)


def _diag_rows(y, n_heads, rows, cols):
    return jnp.concatenate(
        [y[h * rows:(h + 1) * rows, h * cols:(h + 1) * cols] for h in range(n_heads)], axis=0)


def _head_select(idx, vals):
    out = jnp.full(idx.shape, vals[-1], F32)
    for h in range(len(vals) - 2, -1, -1):
        out = jnp.where(idx == h, vals[h], out)
    return out


def _swap_halves(x, dk):
    half = dk // 2
    parts = []
    for c in range(x.shape[1] // LANE):
        xs = x[:, c * LANE:(c + 1) * LANE]
        lo = _iota(xs.shape, 1) % dk < half
        parts.append(jnp.where(lo, pltpu.roll(xs, LANE - half, 1), pltpu.roll(xs, half, 1)))
    return jnp.concatenate(parts, axis=1)


def _rope(x, cos, sin_signed, dk):
    return x * cos + _swap_halves(x, dk) * sin_signed


def _head_rmsnorm(o, n_heads, d):
    parts = []
    for h in range(n_heads):
        oh = o[:, h * d:(h + 1) * d]
        parts.append(oh * lax.rsqrt(jnp.mean(oh * oh, axis=-1, keepdims=True) + EPS))
    return jnp.concatenate(parts, axis=1)


def _head_groupnorm(o, n_heads, d):
    parts = []
    for h in range(n_heads):
        oh = o[:, h * d:(h + 1) * d]
        c = oh - jnp.mean(oh, axis=-1, keepdims=True)
        parts.append(c * lax.rsqrt(jnp.mean(c * c, axis=-1, keepdims=True) + EPS))
    return jnp.concatenate(parts, axis=1)


def _params(*sem):
    return pltpu.CompilerParams(dimension_semantics=sem, vmem_limit_bytes=VMEM_LIMIT)


def _norm_proj_kernel(x_ref, g_ref, w_ref, o_ref):
    o_ref[...] = _dot(_rms(x_ref[...], g_ref[...]), w_ref[...])


def _norm_proj(x, g, w, tm):
    m = x.shape[0]
    n = w.shape[1]
    return pl.pallas_call(
        _norm_proj_kernel,
        grid=(m // tm,),
        in_specs=[pl.BlockSpec((tm, D_MODEL), lambda i: (i, 0)),
                  pl.BlockSpec((1, D_MODEL), lambda i: (0, 0)),
                  pl.BlockSpec((D_MODEL, n), lambda i: (0, 0))],
        out_specs=pl.BlockSpec((tm, n), lambda i: (i, 0)),
        out_shape=jax.ShapeDtypeStruct((m, n), F32),
        compiler_params=_params("parallel"),
        name="norm_proj",
    )(x, g, w)


FFN_TILE = D_FF // 2


def _out_ffn_kernel(x_ref, o_ref, wo_ref, g_ref, wg_ref, wu_ref, wd_ref, gf_ref, y_ref,
                    hn_ref, acc_ref, *, final):
    j = pl.program_id(1)

    @pl.when(j == 0)
    def _():
        xn = x_ref[...] + _dot(o_ref[...], wo_ref[...])
        acc_ref[...] = xn
        hn_ref[...] = _rms(xn, g_ref[...]).astype(BF16)

    hn = hn_ref[...]
    gate = _dot(hn, wg_ref[...])
    up = _dot(hn, wu_ref[...])
    acc_ref[...] += _dot(_silu(gate) * up, wd_ref[...])

    @pl.when(j == pl.num_programs(1) - 1)
    def _():
        y = acc_ref[...]
        if final:
            y = _rms(y, gf_ref[...])
        y_ref[...] = y


def _out_ffn(x, o, w_out, g_ffn, w_up, w_down, g_final, tm, final):
    m = x.shape[0]
    nj = D_FF // FFN_TILE
    return pl.pallas_call(
        functools.partial(_out_ffn_kernel, final=final),
        grid=(m // tm, nj),
        in_specs=[pl.BlockSpec((tm, D_MODEL), lambda i, j: (i, 0)),
                  pl.BlockSpec((tm, D_MODEL), lambda i, j: (i, 0)),
                  pl.BlockSpec((D_MODEL, D_MODEL), lambda i, j: (0, 0)),
                  pl.BlockSpec((1, D_MODEL), lambda i, j: (0, 0)),
                  pl.BlockSpec((D_MODEL, FFN_TILE), lambda i, j: (0, j)),
                  pl.BlockSpec((D_MODEL, FFN_TILE), lambda i, j: (0, j + nj)),
                  pl.BlockSpec((FFN_TILE, D_MODEL), lambda i, j: (j, 0)),
                  pl.BlockSpec((1, D_MODEL), lambda i, j: (0, 0))],
        out_specs=pl.BlockSpec((tm, D_MODEL), lambda i, j: (i, 0)),
        out_shape=jax.ShapeDtypeStruct((m, D_MODEL), F32),
        scratch_shapes=[pltpu.VMEM((tm, D_MODEL), BF16), pltpu.VMEM((tm, D_MODEL), F32)],
        compiler_params=_params("parallel", "arbitrary"),
        name="out_ffn",
    )(x, o, w_out, g_ffn, w_up, w_up, w_down, g_final)


def _gla_logf(glr, wlr_ref, blr_ref):
    return _log_sigmoid(_dot_f32(glr, wlr_ref[...]) + blr_ref[...]) * (1.0 / GLA_TAU)


def _even_prompt_kernel(p_ref, cos_ref, sin_ref, wlr_ref, blr_ref, gng_ref, gnr_ref,
                        o_ref, sg_ref, sr_ref):
    c = EVEN_CHUNK
    r_sub = GLA_SUB

    @pl.when(pl.program_id(1) == 0)
    def _():
        sg_ref[...] = jnp.zeros_like(sg_ref)
        sr_ref[...] = jnp.zeros_like(sr_ref)

    q = p_ref[:, 0:256] * DK_A ** -0.5
    k = p_ref[:, 256:512]
    v = p_ref[:, 512:1024]
    gg = p_ref[:, 1024:1536]
    logf = _gla_logf(p_ref[:, GATE_OFF:GATE_OFF + LANE], wlr_ref, blr_ref)
    b = _dot_f32(_tri(c), logf)
    s_gla = sg_ref[...]
    inter = _dot(_stack_heads(q * jnp.exp(b), H_A, DK_A), s_gla)
    o_a = jnp.concatenate([inter[h * c:(h + 1) * c] for h in range(H_A)], axis=1)

    ind = (_iota((H_A * DK_A, H_A * DV_A), 0) // DK_A ==
           _iota((H_A * DK_A, H_A * DV_A), 1) // DV_A).astype(BF16)
    row = _iota((r_sub, H_A * DK_A), 0)
    blocks = []
    for i in range(c // r_sub):
        lo = i * r_sub
        qb = q[lo:lo + r_sub]
        bb = b[lo:lo + r_sub]
        ws = []
        for s in range(r_sub):
            e = jnp.exp(jnp.minimum(bb - b[lo + s:lo + s + 1], 0.0))
            ws.append(jnp.where(row >= s, qb * k[lo + s:lo + s + 1] * e, 0.0))
        rep = _dot(jnp.concatenate(ws, axis=0), ind)
        ob = rep[0:r_sub] * v[lo:lo + 1]
        for s in range(1, r_sub):
            ob = ob + rep[s * r_sub:(s + 1) * r_sub] * v[lo + s:lo + s + 1]
        if i > 0:
            ref_b = b[lo - 1:lo]
            qi = qb * jnp.exp(bb - ref_b)
            ki = k[0:lo] * jnp.exp(ref_b - b[0:lo])
            sc = _dot_nt(_stack_heads(qi, H_A, DK_A), ki)
            ob = ob + _diag_lanes(_dot(sc, v[0:lo]), H_A, r_sub, DV_A)
        blocks.append(ob)
    o_a = o_a + jnp.concatenate(blocks, axis=0)
    o_a = _head_rmsnorm(o_a, H_A, DV_A) * gng_ref[...] * _silu(gg)

    b_t = b.T
    b_last = b_t[:, c - 1:c]
    kd_t = k.T * jnp.exp(b_last - b_t)
    upd = _diag_rows(_dot(kd_t, v), H_A, DK_A, DV_A)
    sg_ref[...] = s_gla * jnp.exp(b_last) + upd

    cos = cos_ref[...]
    sin = sin_ref[...]
    rq = _rope(p_ref[:, 1536:1792], cos, sin, DK_B) * DK_B ** -0.5
    rk = _rope(p_ref[:, 1792:2048], cos, sin, DK_B)
    rv = p_ref[:, 2048:2560]
    rg = p_ref[:, 2560:3072]
    s_ret = sr_ref[...]
    t_row = _iota((H_B * c, c), 0)
    lg_row = _head_select(t_row // c, LOG_GAMMA)
    t_loc = (t_row % c).astype(F32)
    s_loc = _iota((H_B * c, c), 1).astype(F32)
    decay = jnp.where(t_loc >= s_loc, jnp.exp(jnp.maximum(t_loc - s_loc, 0.0) * lg_row), 0.0)
    q_dec = jnp.exp((t_loc[:, 0:1] + 1.0) * lg_row[:, 0:1])
    qs = _stack_heads(rq, H_B, DK_B)
    scores = _dot_nt(qs, rk) * decay
    o_b = _diag_lanes(_dot(scores, rv), H_B, c, DV_B)
    inter_b = _dot(qs * q_dec, s_ret)
    o_b = o_b + jnp.concatenate([inter_b[h * c:(h + 1) * c] for h in range(H_B)], axis=1)
    o_b = _head_groupnorm(o_b, H_B, DV_B) * gnr_ref[...] * _silu(rg)

    lg_lane = _head_select(_iota((c, H_B * DK_B), 1) // DK_B, LOG_GAMMA)
    k_dec = jnp.exp((c - 1.0 - _iota((c, H_B * DK_B), 0).astype(F32)) * lg_lane)
    upd_b = _diag_rows(_dot((rk * k_dec).T, rv), H_B, DK_B, DV_B)
    g_chunk = jnp.exp(c * _head_select(_iota((H_B * DK_B, DV_B), 0) // DK_B, LOG_GAMMA))
    sr_ref[...] = s_ret * g_chunk + upd_b

    o_ref[...] = jnp.concatenate([o_a, o_b], axis=1).astype(BF16)


def _even_prompt(proj, cos, sin, w_lr, b_lr, gn_gla, gn_ret, nb, seq):
    c = EVEN_CHUNK
    nc = seq // c
    small = lambda shape: pl.BlockSpec(shape, lambda b, n: (0, 0))
    return pl.pallas_call(
        _even_prompt_kernel,
        grid=(nb, nc),
        in_specs=[pl.BlockSpec((c, D_PROJ), lambda b, n: (b * nc + n, 0)),
                  pl.BlockSpec((c, H_B * DK_B), lambda b, n: (n, 0)),
                  pl.BlockSpec((c, H_B * DK_B), lambda b, n: (n, 0)),
                  small((LANE, H_A * DK_A)), small((1, H_A * DK_A)),
                  small((1, D_HALF)), small((1, D_HALF))],
        out_specs=[pl.BlockSpec((c, D_MODEL), lambda b, n: (b * nc + n, 0)),
                   pl.BlockSpec((None, H_A * DK_A, DV_A), lambda b, n: (b, 0, 0)),
                   pl.BlockSpec((None, H_B * DK_B, DV_B), lambda b, n: (b, 0, 0))],
        out_shape=[jax.ShapeDtypeStruct((nb * seq, D_MODEL), BF16),
                   jax.ShapeDtypeStruct((nb, H_A * DK_A, DV_A), F32),
                   jax.ShapeDtypeStruct((nb, H_B * DK_B, DV_B), F32)],
        compiler_params=_params("parallel", "arbitrary"),
        name="even_prompt",
    )(proj, cos, sin, w_lr, b_lr, gn_gla, gn_ret)


def _rg_gates(xc, wr_ref, br_ref, wi_ref, bi_ref, lam_ref):
    r = _sigmoid(_dot(xc, wr_ref[...]) + br_ref[...])
    i = _sigmoid(_dot(xc, wi_ref[...]) + bi_ref[...])
    log_a = -RG_C * r * _softplus(-lam_ref[...])
    a = jnp.exp(log_a)
    u = jnp.sqrt(jnp.tanh(-log_a) * (a * a + 1.0)) * (i * xc)
    return a, u


def _odd_prompt_kernel(p_ref, big_ref, bfg_ref, gnm_ref, cw_ref, cb_ref, wr_ref, br_ref, wi_ref,
                       bi_ref, lam_ref, o_ref, cst_ref, nst_ref, mst_ref, hst_ref, conv_ref,
                       xpad_ref):
    c = ODD_CHUNK

    @pl.when(pl.program_id(1) == 0)
    def _():
        cst_ref[...] = jnp.zeros_like(cst_ref)
        nst_ref[...] = jnp.zeros_like(nst_ref)
        mst_ref[...] = jnp.zeros_like(mst_ref)
        hst_ref[...] = jnp.zeros_like(hst_ref)
        xpad_ref[0:8, :] = jnp.zeros((8, D_RG), F32)

    gates = p_ref[:, GATE_OFF:GATE_OFF + LANE]
    ig = gates + big_ref[...]
    lf = _log_sigmoid(gates + bfg_ref[...])
    b_all = pltpu.roll(_dot_f32(_tri(c), lf), LANE - H_C, 1)
    a_all = ig - b_all
    a_rows = a_all.T
    causal = _iota((c, c), 0) >= _iota((c, c), 1)
    m_prev = mst_ref[...]
    m_new = m_prev
    lane = _iota((1, LANE), 1)
    outs = []
    for h in range(H_C):
        sl = slice(h * DK_C, (h + 1) * DK_C)
        qh = p_ref[:, sl] * DK_C ** -0.5
        kh = p_ref[:, 512 + h * DK_C:512 + (h + 1) * DK_C]
        vh = p_ref[:, 1024 + h * DV_C:1024 + (h + 1) * DV_C]
        bcol = b_all[:, h:h + 1]
        acol = a_all[:, h:h + 1]
        arow = a_rows[h:h + 1, :]
        mp = m_prev[:, h:h + 1]
        cmax = jnp.max(jnp.where(causal, arow, -jnp.inf), axis=1, keepdims=True)
        m_t = jnp.maximum(bcol + mp, bcol + cmax)
        w_state = jnp.exp(bcol + mp - m_t)
        w_intra = jnp.where(causal, jnp.exp(jnp.minimum(bcol + arow - m_t, 0.0)), 0.0)
        c_h = cst_ref[sl, :]
        n_h = nst_ref[h:h + 1, :]
        scores = _dot_nt(qh, kh) * w_intra
        num = w_state * _dot(qh, c_h) + _dot(scores, vh)
        den = w_state * jnp.sum(qh * n_h, axis=1, keepdims=True) + \
            jnp.sum(scores, axis=1, keepdims=True)
        outs.append(num / jnp.maximum(jnp.abs(den), jnp.exp(-m_t)))
        m_last = m_t[c - 1:c]
        b_last = bcol[c - 1:c]
        w_s_last = jnp.exp(b_last + mp - m_last)
        kw = kh * jnp.exp(b_last + acol - m_last)
        cst_ref[sl, :] = c_h * w_s_last + _dot(kw.T, vh)
        nst_ref[h:h + 1, :] = n_h * w_s_last + jnp.sum(kw, axis=0, keepdims=True)
        m_new = jnp.where(lane == h, m_last, m_new)
    mst_ref[...] = m_new
    o_c = _head_rmsnorm(jnp.concatenate(outs, axis=1), H_C, DV_C) * gnm_ref[...] * \
        _sigmoid(p_ref[:, 1536:2048])

    rx = p_ref[:, 2048:2560]
    xpad_ref[8:8 + c, :] = rx
    xc = cb_ref[...] + cw_ref[3:4, :] * rx
    for j in range(1, CONV_W):
        xc = xc + cw_ref[3 - j:4 - j, :] * xpad_ref[8 - j:8 - j + c, :]
    conv_ref[...] = xpad_ref[c + 8 - (CONV_W - 1):c + 8, :]
    xpad_ref[0:8, :] = xpad_ref[c:c + 8, :]
    a, u = _rg_gates(xc, wr_ref, br_ref, wi_ref, bi_ref, lam_ref)
    t_idx = _iota((c, D_RG), 0)
    s = 1
    while s < c:
        keep = t_idx >= s
        u = jnp.where(keep, a * pltpu.roll(u, s, 0) + u, u)
        a = jnp.where(keep, a * pltpu.roll(a, s, 0), a)
        s *= 2
    hs = a * hst_ref[...] + u
    hst_ref[...] = hs[c - 1:c]
    o_d = hs * _gelu_tanh(p_ref[:, 2560:3072])

    o_ref[...] = jnp.concatenate([o_c, o_d], axis=1).astype(BF16)


def _odd_prompt(proj, b_ig, b_fg, gn_ml, conv_w, conv_b, w_r, b_r, w_i, b_i, lam, nb, seq):
    c = ODD_CHUNK
    nc = seq // c
    small = lambda shape: pl.BlockSpec(shape, lambda b, n: (0, 0))
    return pl.pallas_call(
        _odd_prompt_kernel,
        grid=(nb, nc),
        in_specs=[pl.BlockSpec((c, D_PROJ), lambda b, n: (b * nc + n, 0)),
                  small((1, LANE)), small((1, LANE)), small((1, D_HALF)),
                  small((CONV_W, D_RG)), small((1, D_RG)),
                  small((D_RG, D_RG)), small((1, D_RG)), small((D_RG, D_RG)), small((1, D_RG)),
                  small((1, D_RG))],
        out_specs=[pl.BlockSpec((c, D_MODEL), lambda b, n: (b * nc + n, 0)),
                   pl.BlockSpec((None, H_C * DK_C, DV_C), lambda b, n: (b, 0, 0)),
                   pl.BlockSpec((None, H_C, DK_C), lambda b, n: (b, 0, 0)),
                   pl.BlockSpec((None, 1, LANE), lambda b, n: (b, 0, 0)),
                   pl.BlockSpec((None, 1, D_RG), lambda b, n: (b, 0, 0)),
                   pl.BlockSpec((None, CONV_W - 1, D_RG), lambda b, n: (b, 0, 0))],
        out_shape=[jax.ShapeDtypeStruct((nb * seq, D_MODEL), BF16),
                   jax.ShapeDtypeStruct((nb, H_C * DK_C, DV_C), F32),
                   jax.ShapeDtypeStruct((nb, H_C, DK_C), F32),
                   jax.ShapeDtypeStruct((nb, 1, LANE), F32),
                   jax.ShapeDtypeStruct((nb, 1, D_RG), F32),
                   jax.ShapeDtypeStruct((nb, CONV_W - 1, D_RG), F32)],
        scratch_shapes=[pltpu.VMEM((c + 8, D_RG), F32)],
        compiler_params=_params("parallel", "arbitrary"),
        name="odd_prompt",
    )(proj, b_ig, b_fg, gn_ml, conv_w, conv_b, w_r, b_r, w_i, b_i, lam)


def _rank1_step(s_in_ref, s_out_ref, o_scr_ref, q_t, k_t, d_t, v, n_heads, dk, dv):
    bb = v.shape[0]
    for j in range(bb):
        vst = jnp.concatenate(
            [jnp.broadcast_to(v[j:j + 1, h * dv:(h + 1) * dv], (dk, dv)) for h in range(n_heads)],
            axis=0)
        s_new = s_in_ref[j] * d_t[:, j:j + 1] + k_t[:, j:j + 1] * vst
        s_out_ref[j] = s_new
        prod = q_t[:, j:j + 1] * s_new
        o_scr_ref[j:j + 1, :] = jnp.concatenate(
            [jnp.sum(prod[h * dk:(h + 1) * dk], axis=0, keepdims=True) for h in range(n_heads)],
            axis=1)


def _even_step_kernel(p_ref, cos_ref, sin_ref, wlr_ref, blr_ref, gng_ref, gnr_ref, sg_in_ref,
                      sr_in_ref, o_ref, sg_out_ref, sr_out_ref, oa_ref, ob_ref):
    bb = STEP_BB
    q = p_ref[:, 0:256] * DK_A ** -0.5
    k = p_ref[:, 256:512]
    v = p_ref[:, 512:1024]
    gg = p_ref[:, 1024:1536]
    d = jnp.exp(_gla_logf(p_ref[:, GATE_OFF:GATE_OFF + LANE], wlr_ref, blr_ref))
    _rank1_step(sg_in_ref, sg_out_ref, oa_ref, q.T, k.T, d.T, v, H_A, DK_A, DV_A)
    o_a = _head_rmsnorm(oa_ref[...], H_A, DV_A) * gng_ref[...] * _silu(gg)

    cos = cos_ref[...]
    sin = sin_ref[...]
    rq = _rope(p_ref[:, 1536:1792], cos, sin, DK_B) * DK_B ** -0.5
    rk = _rope(p_ref[:, 1792:2048], cos, sin, DK_B)
    rv = p_ref[:, 2048:2560]
    rg = p_ref[:, 2560:3072]
    gamma_t = jnp.exp(_head_select(_iota((H_B * DK_B, bb), 0) // DK_B, LOG_GAMMA))
    _rank1_step(sr_in_ref, sr_out_ref, ob_ref, rq.T, rk.T, gamma_t, rv, H_B, DK_B, DV_B)
    o_b = _head_groupnorm(ob_ref[...], H_B, DV_B) * gnr_ref[...] * _silu(rg)
    o_ref[...] = jnp.concatenate([o_a, o_b], axis=1).astype(BF16)


def _even_step(proj, cos, sin, w_lr, b_lr, gn_gla, gn_ret, s_gla, s_ret):
    nb = proj.shape[0]
    bb = STEP_BB
    small = lambda shape: pl.BlockSpec(shape, lambda i: (0, 0))
    state = lambda rows, cols: pl.BlockSpec((bb, rows, cols), lambda i: (i, 0, 0))
    return pl.pallas_call(
        _even_step_kernel,
        grid=(nb // bb,),
        in_specs=[pl.BlockSpec((bb, D_PROJ), lambda i: (i, 0)),
                  small((1, H_B * DK_B)), small((1, H_B * DK_B)),
                  small((LANE, H_A * DK_A)), small((1, H_A * DK_A)),
                  small((1, D_HALF)), small((1, D_HALF)),
                  state(H_A * DK_A, DV_A), state(H_B * DK_B, DV_B)],
        out_specs=[pl.BlockSpec((bb, D_MODEL), lambda i: (i, 0)),
                   state(H_A * DK_A, DV_A), state(H_B * DK_B, DV_B)],
        out_shape=[jax.ShapeDtypeStruct((nb, D_MODEL), BF16),
                   jax.ShapeDtypeStruct(s_gla.shape, F32),
                   jax.ShapeDtypeStruct(s_ret.shape, F32)],
        scratch_shapes=[pltpu.VMEM((bb, D_HALF), F32), pltpu.VMEM((bb, D_HALF), F32)],
        compiler_params=_params("parallel"),
        name="even_step",
    )(proj, cos, sin, w_lr, b_lr, gn_gla, gn_ret, s_gla, s_ret)


def _odd_step_kernel(p_ref, big_ref, bfg_ref, gnm_ref, cw_ref, cb_ref, wr_ref, br_ref, wi_ref,
                     bi_ref, lam_ref, c_in_ref, n_in_ref, m_in_ref, h_in_ref, conv_in_ref,
                     o_ref, c_out_ref, n_out_ref, m_out_ref, h_out_ref, conv_out_ref, num_ref):
    bb = STEP_BB
    gates = p_ref[:, GATE_OFF:GATE_OFF + LANE]
    ig = gates + big_ref[...]
    lf = pltpu.roll(_log_sigmoid(gates + bfg_ref[...]), LANE - H_C, 1)
    m_prev = m_in_ref[...]
    m_new = jnp.maximum(lf + m_prev, ig)
    valid = _iota((bb, LANE), 1) < H_C
    w_s = jnp.where(valid, jnp.exp(lf + m_prev - m_new), 0.0)
    w_i = jnp.where(valid, jnp.exp(ig - m_new), 0.0)
    m_out_ref[...] = jnp.where(valid, m_new, 0.0)
    expand = (_iota((LANE, H_C * DK_C), 0) == _iota((LANE, H_C * DK_C), 1) // DK_C).astype(F32)
    w_s_wide = _dot_f32(w_s, expand)
    w_i_wide = _dot_f32(w_i, expand)
    q = p_ref[:, 0:512] * DK_C ** -0.5
    kw = p_ref[:, 512:1024] * w_i_wide
    v = p_ref[:, 1024:1536]
    _rank1_step(c_in_ref, c_out_ref, num_ref, q.T, kw.T, w_s_wide.T, v, H_C, DK_C, DV_C)
    n_new = n_in_ref[...] * w_s_wide + kw
    n_out_ref[...] = n_new
    num = num_ref[...]
    qn = q * n_new
    outs = []
    for h in range(H_C):
        sl = slice(h * DK_C, (h + 1) * DK_C)
        den = jnp.sum(qn[:, sl], axis=1, keepdims=True)
        outs.append(num[:, sl] / jnp.maximum(jnp.abs(den), jnp.exp(-m_new[:, h:h + 1])))
    o_c = _head_rmsnorm(jnp.concatenate(outs, axis=1), H_C, DV_C) * gnm_ref[...] * \
        _sigmoid(p_ref[:, 1536:2048])

    rx = p_ref[:, 2048:2560]
    xc = cb_ref[...] + cw_ref[3:4, :] * rx
    for j in range(CONV_W - 1):
        xc = xc + cw_ref[j:j + 1, :] * conv_in_ref[j]
    for j in range(CONV_W - 2):
        conv_out_ref[j] = conv_in_ref[j + 1]
    conv_out_ref[CONV_W - 2] = rx
    a, u = _rg_gates(xc, wr_ref, br_ref, wi_ref, bi_ref, lam_ref)
    hs = a * h_in_ref[...] + u
    h_out_ref[...] = hs
    o_d = hs * _gelu_tanh(p_ref[:, 2560:3072])
    o_ref[...] = jnp.concatenate([o_c, o_d], axis=1).astype(BF16)


def _odd_step(proj, b_ig, b_fg, gn_ml, conv_w, conv_b, w_r, b_r, w_i, b_i, lam,
              s_c, s_n, s_m, s_h, s_conv):
    nb = proj.shape[0]
    bb = STEP_BB
    small = lambda shape: pl.BlockSpec(shape, lambda i: (0, 0))
    rows = lambda cols: pl.BlockSpec((bb, cols), lambda i: (i, 0))
    c_spec = pl.BlockSpec((bb, H_C * DK_C, DV_C), lambda i: (i, 0, 0))
    conv_spec = pl.BlockSpec((CONV_W - 1, bb, D_RG), lambda i: (0, i, 0))
    return pl.pallas_call(
        _odd_step_kernel,
        grid=(nb // bb,),
        in_specs=[rows(D_PROJ),
                  small((1, LANE)), small((1, LANE)), small((1, D_HALF)),
                  small((CONV_W, D_RG)), small((1, D_RG)),
                  small((D_RG, D_RG)), small((1, D_RG)), small((D_RG, D_RG)), small((1, D_RG)),
                  small((1, D_RG)),
                  c_spec, rows(H_C * DK_C), rows(LANE), rows(D_RG), conv_spec],
        out_specs=[rows(D_MODEL), c_spec, rows(H_C * DK_C), rows(LANE), rows(D_RG), conv_spec],
        out_shape=[jax.ShapeDtypeStruct((nb, D_MODEL), BF16),
                   jax.ShapeDtypeStruct(s_c.shape, F32),
                   jax.ShapeDtypeStruct(s_n.shape, F32),
                   jax.ShapeDtypeStruct(s_m.shape, F32),
                   jax.ShapeDtypeStruct(s_h.shape, F32),
                   jax.ShapeDtypeStruct(s_conv.shape, F32)],
        scratch_shapes=[pltpu.VMEM((bb, D_HALF), F32)],
        compiler_params=_params("parallel"),
        name="odd_step",
    )(proj, b_ig, b_fg, gn_ml, conv_w, conv_b, w_r, b_r, w_i, b_i, lam, s_c, s_n, s_m, s_h, s_conv)


def _rope_tables(pos):
    half = DK_B // 2
    inv = ROPE_BASE ** (-jnp.arange(half, dtype=F32) / half)
    ang = pos.astype(F32)[:, None] * inv[None, :]
    cos = jnp.tile(jnp.cos(ang), (1, 2 * H_B))
    sin = jnp.tile(jnp.concatenate([-jnp.sin(ang), jnp.sin(ang)], axis=1), (1, H_B))
    return cos, sin


def _reorder_cols(w, order, width):
    w = w[:, np.concatenate(order)]
    return jnp.pad(w, ((0, 0), (0, width - w.shape[1]))).astype(BF16)


def _pad_lanes(x, offset=0):
    return jnp.pad(x.reshape(1, -1), ((0, 0), (offset, LANE - offset - x.size)))


def _block_diag(w):
    return jax.scipy.linalg.block_diag(*[w[i] for i in range(RG_BLOCKS)]).astype(BF16)


def kernel(x_prompt, x_sample, state_gla, state_ret, state_mlstm_C, state_mlstm_n, state_mlstm_m, state_rglru_h, state_rglru_conv, norm_mix, norm_ffn, norm_final, even_w_in, even_w_lr, even_b_lr, even_gn_gla, even_gn_ret, even_w_out, odd_w_in, ml_b_i, ml_b_f, odd_gn_ml, rg_conv_w, rg_conv_b, rg_w_r, rg_b_r, rg_w_i, rg_b_i, rg_lam, odd_w_out, ffn_w_up, ffn_w_down):
    nb, seq, _ = x_prompt.shape
    ns = x_sample.shape[0]
    row = lambda x: x.reshape(1, -1)

    ar = np.arange
    even_order = [ar(0, 1536), ar(1552, 3088), ar(1536, 1552)]
    odd_order = [ar(0, 1536), ar(1544, 3080), ar(1536, 1544)]
    cos_p, sin_p = _rope_tables(jnp.arange(seq, dtype=jnp.int32))
    cos_s, sin_s = _rope_tables(PAST_LEN + jnp.arange(1, dtype=jnp.int32))

    xp = x_prompt.reshape(nb * seq, D_MODEL)
    xs = x_sample.reshape(ns, D_MODEL)
    st = {k: [] for k in ("gla_p", "ret_p", "c_p", "n_p", "m_p", "h_p", "conv_p",
                          "gla_s", "ret_s", "c_s", "n_s", "m_s", "h_s", "conv_s")}
    for layer in range(DEPTH):
        g_mix = row(norm_mix[layer])
        final = layer == DEPTH - 1
        if layer % 2 == 0:
            e = layer // 2
            w_in = _reorder_cols(even_w_in[e], even_order, D_PROJ)
            w_out = even_w_out[e].astype(BF16)
            w_lr = jnp.pad(even_w_lr[e], ((0, LANE - GLA_RANK), (0, 0)))
            small = (w_lr, row(even_b_lr[e]), row(even_gn_gla[e]), row(even_gn_ret[e]))
            o_p, sg, sr = _even_prompt(_norm_proj(xp, g_mix, w_in, 512), cos_p, sin_p, *small,
                                       nb, seq)
            st["gla_p"].append(sg.reshape(nb, H_A, DK_A, DV_A))
            st["ret_p"].append(sr.reshape(nb, H_B, DK_B, DV_B))
            o_s, sg, sr = _even_step(_norm_proj(xs, g_mix, w_in, ns), cos_s, sin_s, *small,
                                     state_gla[e].reshape(ns, H_A * DK_A, DV_A),
                                     state_ret[e].reshape(ns, H_B * DK_B, DV_B))
            st["gla_s"].append(sg.reshape(ns, H_A, DK_A, DV_A))
            st["ret_s"].append(sr.reshape(ns, H_B, DK_B, DV_B))
        else:
            o = layer // 2
            w_in = _reorder_cols(odd_w_in[o], odd_order, D_PROJ)
            w_out = odd_w_out[o].astype(BF16)
            small = (_pad_lanes(ml_b_i[o]), _pad_lanes(ml_b_f[o], H_C), row(odd_gn_ml[o]),
                     rg_conv_w[o], row(rg_conv_b[o]), _block_diag(rg_w_r[o]), row(rg_b_r[o]),
                     _block_diag(rg_w_i[o]), row(rg_b_i[o]), row(rg_lam[o]))
            o_p, c_, n_, m_, h_, cv = _odd_prompt(_norm_proj(xp, g_mix, w_in, 512), *small,
                                                  nb, seq)
            st["c_p"].append(c_.reshape(nb, H_C, DK_C, DV_C))
            st["n_p"].append(n_)
            st["m_p"].append(m_[:, 0, :H_C])
            st["h_p"].append(h_[:, 0, :])
            st["conv_p"].append(cv)
            o_s, c_, n_, m_, h_, cv = _odd_step(
                _norm_proj(xs, g_mix, w_in, ns), *small,
                state_mlstm_C[o].reshape(ns, H_C * DK_C, DV_C),
                state_mlstm_n[o].reshape(ns, H_C * DK_C),
                jnp.pad(state_mlstm_m[o], ((0, 0), (0, LANE - H_C))),
                state_rglru_h[o],
                jnp.swapaxes(state_rglru_conv[o], 0, 1))
            st["c_s"].append(c_.reshape(ns, H_C, DK_C, DV_C))
            st["n_s"].append(n_.reshape(ns, H_C, DK_C))
            st["m_s"].append(m_[:, :H_C])
            st["h_s"].append(h_)
            st["conv_s"].append(jnp.swapaxes(cv, 0, 1))
        ffn = (row(norm_ffn[layer]), ffn_w_up[layer].astype(BF16), ffn_w_down[layer].astype(BF16),
               row(norm_final))
        xp = _out_ffn(xp, o_p, w_out, *ffn, 512, final)
        xs = _out_ffn(xs, o_s, w_out, *ffn, ns, final)

    stack = lambda name: jnp.stack(st[name])
    return (xp.reshape(nb, seq, D_MODEL), xs.reshape(ns, 1, D_MODEL),
            stack("gla_p"), stack("ret_p"), stack("c_p"), stack("n_p"), stack("m_p"),
            stack("h_p"), stack("conv_p"),
            stack("gla_s"), stack("ret_s"), stack("c_s"), stack("n_s"), stack("m_s"),
            stack("h_s"), stack("conv_s"))
```

```python
import functools

import numpy as np
import jax
import jax.numpy as jnp
from jax import lax
from jax.experimental import pallas as pl
from jax.experimental.pallas import tpu as pltpu

F32 = jnp.float32
BF16 = jnp.bfloat16

D_MODEL = 1024
DEPTH = 4
PAST_LEN = 16384
N_EVEN = (DEPTH + 1) // 2
N_ODD = DEPTH // 2
D_HALF = D_MODEL // 2
H_A = 4
DV_A = D_HALF // H_A
DK_A = DV_A // 2
GLA_RANK = 16
GLA_TAU = 16.0
H_B = 4
DV_B = D_HALF // H_B
DK_B = DV_B // 2
ROPE_BASE = 10000.0
H_C = 4
DK_C = D_HALF // H_C
DV_C = D_HALF // H_C
D_RG = D_HALF
RG_BLOCKS = 8
RG_BW = D_RG // RG_BLOCKS
RG_C = 8.0
CONV_W = 4
D_FF = (8 * D_MODEL + 3 * 256 - 1) // (3 * 256) * 256
EPS = 1e-6

LANE = 128
D_PROJ = 3072 + LANE
GATE_OFF = 3072
EVEN_CHUNK = 64
GLA_SUB = 16
ODD_CHUNK = 128
STEP_BB = 8
VMEM_LIMIT = 56 * 1024 * 1024

LOG_GAMMA = [float(np.log1p(-np.exp2(-5.0 - h))) for h in range(H_B)]


def _dot(a, b):
    return jnp.dot(a.astype(BF16), b.astype(BF16), preferred_element_type=F32)


def _dot_nt(a, b):
    return lax.dot_general(a.astype(BF16), b.astype(BF16), (((1,), (1,)), ((), ())),
                           preferred_element_type=F32)


def _dot_f32(a, b):
    return jnp.dot(a, b, preferred_element_type=F32, precision=lax.Precision.HIGHEST)


def _rms(x, g):
    return x * lax.rsqrt(jnp.mean(x * x, axis=-1, keepdims=True) + EPS) * g


def _softplus(x):
    return jnp.maximum(x, 0.0) + jnp.log1p(jnp.exp(-jnp.abs(x)))


def _log_sigmoid(x):
    return -_softplus(-x)


def _sigmoid(x):
    return 1.0 / (1.0 + jnp.exp(-x))


def _silu(x):
    return x * _sigmoid(x)


def _gelu_tanh(x):
    return 0.5 * x * (1.0 + jnp.tanh(0.7978845608028654 * (x + 0.044715 * x * x * x)))


def _iota(shape, dim):
    return lax.broadcasted_iota(jnp.int32, shape, dim)


def _tri(c):
    return (_iota((c, c), 0) >= _iota((c, c), 1)).astype(F32)


def _stack_heads(x, n_heads, dk):
    head = _iota(x.shape, 1) // dk
    return jnp.concatenate([jnp.where(head == h, x, 0.0) for h in range(n_heads)], axis=0)


def _diag_lanes(y, n_heads, rows, cols):
    return jnp.concatenate(
        [y[h * rows:(h + 1) * rows, h * cols:(h + 1) * cols] for h in range(n_heads)], axis=1)


def _diag_rows(y, n_heads, rows, cols):
    return jnp.concatenate(
        [y[h * rows:(h + 1) * rows, h * cols:(h + 1) * cols] for h in range(n_heads)], axis=0)


def _head_select(idx, vals):
    out = jnp.full(idx.shape, vals[-1], F32)
    for h in range(len(vals) - 2, -1, -1):
        out = jnp.where(idx == h, vals[h], out)
    return out


def _swap_halves(x, dk):
    half = dk // 2
    parts = []
    for c in range(x.shape[1] // LANE):
        xs = x[:, c * LANE:(c + 1) * LANE]
        lo = _iota(xs.shape, 1) % dk < half
        parts.append(jnp.where(lo, pltpu.roll(xs, LANE - half, 1), pltpu.roll(xs, half, 1)))
    return jnp.concatenate(parts, axis=1)


def _rope(x, cos, sin_signed, dk):
    return x * cos + _swap_halves(x, dk) * sin_signed


def _head_rmsnorm(o, n_heads, d):
    parts = []
    for h in range(n_heads):
        oh = o[:, h * d:(h + 1) * d]
        parts.append(oh * lax.rsqrt(jnp.mean(oh * oh, axis=-1, keepdims=True) + EPS))
    return jnp.concatenate(parts, axis=1)


def _head_groupnorm(o, n_heads, d):
    parts = []
    for h in range(n_heads):
        oh = o[:, h * d:(h + 1) * d]
        c = oh - jnp.mean(oh, axis=-1, keepdims=True)
        parts.append(c * lax.rsqrt(jnp.mean(c * c, axis=-1, keepdims=True) + EPS))
    return jnp.concatenate(parts, axis=1)


def _params(*sem):
    return pltpu.CompilerParams(dimension_semantics=sem, vmem_limit_bytes=VMEM_LIMIT)


def _norm_proj_kernel(x_ref, g_ref, w_ref, o_ref):
    o_ref[...] = _dot(_rms(x_ref[...], g_ref[...]), w_ref[...])


def _resident(shape, index):
    return pl.BlockSpec(shape, lambda i: index, pipeline_mode=pl.Buffered(1))


def _norm_proj(x, g, w_stack, idx, tm):
    m = x.shape[0]
    n = w_stack.shape[2]
    return pl.pallas_call(
        _norm_proj_kernel,
        grid=(m // tm,),
        in_specs=[pl.BlockSpec((tm, D_MODEL), lambda i: (i, 0)),
                  _resident((1, D_MODEL), (0, 0)),
                  _resident((None, D_MODEL, n), (idx, 0, 0))],
        out_specs=pl.BlockSpec((tm, n), lambda i: (i, 0)),
        out_shape=jax.ShapeDtypeStruct((m, n), F32),
        compiler_params=_params("parallel"),
        name="norm_proj",
    )(x, g, w_stack)


def _out_ffn_kernel(x_ref, o_ref, wo_ref, g_ref, wg_ref, wu_ref, wd_ref, gf_ref, y_ref, *, final):
    xn = x_ref[...] + _dot(o_ref[...], wo_ref[...])
    hn = _rms(xn, g_ref[...]).astype(BF16)
    gate = _dot(hn, wg_ref[...])
    up = _dot(hn, wu_ref[...])
    y = xn + _dot(_silu(gate) * up, wd_ref[...])
    if final:
        y = _rms(y, gf_ref[...])
    y_ref[...] = y


def _out_ffn(x, o, w_out_stack, idx, g_ffn, w_up_stack, w_down_stack, layer, g_final, tm, final):
    m = x.shape[0]
    return pl.pallas_call(
        functools.partial(_out_ffn_kernel, final=final),
        grid=(m // tm,),
        in_specs=[pl.BlockSpec((tm, D_MODEL), lambda i: (i, 0)),
                  pl.BlockSpec((tm, D_MODEL), lambda i: (i, 0)),
                  _resident((None, D_MODEL, D_MODEL), (idx, 0, 0)),
                  _resident((1, D_MODEL), (0, 0)),
                  _resident((None, D_MODEL, D_FF), (layer, 0, 0)),
                  _resident((None, D_MODEL, D_FF), (layer, 0, 1)),
                  _resident((None, D_FF, D_MODEL), (layer, 0, 0)),
                  _resident((1, D_MODEL), (0, 0))],
        out_specs=pl.BlockSpec((tm, D_MODEL), lambda i: (i, 0)),
        out_shape=jax.ShapeDtypeStruct((m, D_MODEL), F32),
        compiler_params=_params("parallel"),
        name="out_ffn",
    )(x, o, w_out_stack, g_ffn, w_up_stack, w_up_stack, w_down_stack, g_final)


def _gla_logf(glr, wlr_ref, blr_ref):
    return _log_sigmoid(_dot_f32(glr, wlr_ref[...]) + blr_ref[...]) * (1.0 / GLA_TAU)


def _even_prompt_kernel(p_ref, cos_ref, sin_ref, wlr_ref, blr_ref, gng_ref, gnr_ref,
                        o_ref, sg_ref, sr_ref):
    c = EVEN_CHUNK
    r_sub = GLA_SUB

    @pl.when(pl.program_id(1) == 0)
    def _():
        sg_ref[...] = jnp.zeros_like(sg_ref)
        sr_ref[...] = jnp.zeros_like(sr_ref)

    q = p_ref[:, 0:256] * DK_A ** -0.5
    k = p_ref[:, 256:512]
    v = p_ref[:, 512:1024]
    gg = p_ref[:, 1024:1536]
    logf = _gla_logf(p_ref[:, GATE_OFF:GATE_OFF + LANE], wlr_ref, blr_ref)
    b = _dot_f32(_tri(c), logf)
    s_gla = sg_ref[...]
    inter = _dot(_stack_heads(q * jnp.exp(b), H_A, DK_A), s_gla)
    o_a = jnp.concatenate([inter[h * c:(h + 1) * c] for h in range(H_A)], axis=1)

    ind = (_iota((H_A * DK_A, H_A * DV_A), 0) // DK_A ==
           _iota((H_A * DK_A, H_A * DV_A), 1) // DV_A).astype(BF16)
    row = _iota((r_sub, H_A * DK_A), 0)
    blocks = []
    for i in range(c // r_sub):
        lo = i * r_sub
        qb = q[lo:lo + r_sub]
        bb = b[lo:lo + r_sub]
        ws = []
        for s in range(r_sub):
            e = jnp.exp(jnp.minimum(bb - b[lo + s:lo + s + 1], 0.0))
            ws.append(jnp.where(row >= s, qb * k[lo + s:lo + s + 1] * e, 0.0))
        rep = _dot(jnp.concatenate(ws, axis=0), ind)
        ob = rep[0:r_sub] * v[lo:lo + 1]
        for s in range(1, r_sub):
            ob = ob + rep[s * r_sub:(s + 1) * r_sub] * v[lo + s:lo + s + 1]
        if i > 0:
            ref_b = b[lo - 1:lo]
            qi = qb * jnp.exp(bb - ref_b)
            ki = k[0:lo] * jnp.exp(ref_b - b[0:lo])
            sc = _dot_nt(_stack_heads(qi, H_A, DK_A), ki)
            ob = ob + _diag_lanes(_dot(sc, v[0:lo]), H_A, r_sub, DV_A)
        blocks.append(ob)
    o_a = o_a + jnp.concatenate(blocks, axis=0)
    o_a = _head_rmsnorm(o_a, H_A, DV_A) * gng_ref[...] * _silu(gg)

    b_t = b.T
    b_last = b_t[:, c - 1:c]
    kd_t = k.T * jnp.exp(b_last - b_t)
    upd = _diag_rows(_dot(kd_t, v), H_A, DK_A, DV_A)
    sg_ref[...] = s_gla * jnp.exp(b_last) + upd

    cos = cos_ref[...]
    sin = sin_ref[...]
    rq = _rope(p_ref[:, 1536:1792], cos, sin, DK_B) * DK_B ** -0.5
    rk = _rope(p_ref[:, 1792:2048], cos, sin, DK_B)
    rv = p_ref[:, 2048:2560]
    rg = p_ref[:, 2560:3072]
    s_ret = sr_ref[...]
    t_row = _iota((H_B * c, c), 0)
    lg_row = _head_select(t_row // c, LOG_GAMMA)
    t_loc = (t_row % c).astype(F32)
    s_loc = _iota((H_B * c, c), 1).astype(F32)
    decay = jnp.where(t_loc >= s_loc, jnp.exp(jnp.maximum(t_loc - s_loc, 0.0) * lg_row), 0.0)
    q_dec = jnp.exp((t_loc[:, 0:1] + 1.0) * lg_row[:, 0:1])
    qs = _stack_heads(rq, H_B, DK_B)
    scores = _dot_nt(qs, rk) * decay
    o_b = _diag_lanes(_dot(scores, rv), H_B, c, DV_B)
    inter_b = _dot(qs * q_dec, s_ret)
    o_b = o_b + jnp.concatenate([inter_b[h * c:(h + 1) * c] for h in range(H_B)], axis=1)
    o_b = _head_groupnorm(o_b, H_B, DV_B) * gnr_ref[...] * _silu(rg)

    lg_lane = _head_select(_iota((c, H_B * DK_B), 1) // DK_B, LOG_GAMMA)
    k_dec = jnp.exp((c - 1.0 - _iota((c, H_B * DK_B), 0).astype(F32)) * lg_lane)
    upd_b = _diag_rows(_dot((rk * k_dec).T, rv), H_B, DK_B, DV_B)
    g_chunk = jnp.exp(c * _head_select(_iota((H_B * DK_B, DV_B), 0) // DK_B, LOG_GAMMA))
    sr_ref[...] = s_ret * g_chunk + upd_b

    o_ref[...] = jnp.concatenate([o_a, o_b], axis=1).astype(BF16)


def _even_prompt(proj, cos, sin, w_lr, b_lr, gn_gla, gn_ret, nb, seq):
    c = EVEN_CHUNK
    nc = seq // c
    small = lambda shape: pl.BlockSpec(shape, lambda b, n: (0, 0))
    return pl.pallas_call(
        _even_prompt_kernel,
        grid=(nb, nc),
        in_specs=[pl.BlockSpec((c, D_PROJ), lambda b, n: (b * nc + n, 0)),
                  pl.BlockSpec((c, H_B * DK_B), lambda b, n: (n, 0)),
                  pl.BlockSpec((c, H_B * DK_B), lambda b, n: (n, 0)),
                  small((LANE, H_A * DK_A)), small((1, H_A * DK_A)),
                  small((1, D_HALF)), small((1, D_HALF))],
        out_specs=[pl.BlockSpec((c, D_MODEL), lambda b, n: (b * nc + n, 0)),
                   pl.BlockSpec((None, H_A * DK_A, DV_A), lambda b, n: (b, 0, 0)),
                   pl.BlockSpec((None, H_B * DK_B, DV_B), lambda b, n: (b, 0, 0))],
        out_shape=[jax.ShapeDtypeStruct((nb * seq, D_MODEL), BF16),
                   jax.ShapeDtypeStruct((nb, H_A * DK_A, DV_A), F32),
                   jax.ShapeDtypeStruct((nb, H_B * DK_B, DV_B), F32)],
        compiler_params=_params("parallel", "arbitrary"),
        name="even_prompt",
    )(proj, cos, sin, w_lr, b_lr, gn_gla, gn_ret)


def _rg_gates(xc, wr_ref, br_ref, wi_ref, bi_ref, lam_ref):
    r = _sigmoid(_dot(xc, wr_ref[...]) + br_ref[...])
    i = _sigmoid(_dot(xc, wi_ref[...]) + bi_ref[...])
    log_a = -RG_C * r * _softplus(-lam_ref[...])
    a = jnp.exp(log_a)
    u = jnp.sqrt(jnp.tanh(-log_a) * (a * a + 1.0)) * (i * xc)
    return a, u


def _odd_prompt_kernel(p_ref, big_ref, bfg_ref, gnm_ref, cw_ref, cb_ref, wr_ref, br_ref, wi_ref,
                       bi_ref, lam_ref, o_ref, cst_ref, nst_ref, mst_ref, hst_ref, conv_ref,
                       xpad_ref):
    c = ODD_CHUNK

    @pl.when(pl.program_id(1) == 0)
    def _():
        cst_ref[...] = jnp.zeros_like(cst_ref)
        nst_ref[...] = jnp.zeros_like(nst_ref)
        mst_ref[...] = jnp.zeros_like(mst_ref)
        hst_ref[...] = jnp.zeros_like(hst_ref)
        xpad_ref[0:8, :] = jnp.zeros((8, D_RG), F32)

    gates = p_ref[:, GATE_OFF:GATE_OFF + LANE]
    ig = gates + big_ref[...]
    lf = _log_sigmoid(gates + bfg_ref[...])
    b_all = pltpu.roll(_dot_f32(_tri(c), lf), LANE - H_C, 1)
    a_all = ig - b_all
    a_rows = a_all.T
    causal = _iota((c, c), 0) >= _iota((c, c), 1)
    m_prev = mst_ref[...]
    m_new = m_prev
    lane = _iota((1, LANE), 1)
    outs = []
    for h in range(H_C):
        sl = slice(h * DK_C, (h + 1) * DK_C)
        qh = p_ref[:, sl] * DK_C ** -0.5
        kh = p_ref[:, 512 + h * DK_C:512 + (h + 1) * DK_C]
        vh = p_ref[:, 1024 + h * DV_C:1024 + (h + 1) * DV_C]
        bcol = b_all[:, h:h + 1]
        acol = a_all[:, h:h + 1]
        arow = a_rows[h:h + 1, :]
        mp = m_prev[:, h:h + 1]
        cmax = jnp.max(jnp.where(causal, arow, -jnp.inf), axis=1, keepdims=True)
        m_t = jnp.maximum(bcol + mp, bcol + cmax)
        w_state = jnp.exp(bcol + mp - m_t)
        w_intra = jnp.where(causal, jnp.exp(jnp.minimum(bcol + arow - m_t, 0.0)), 0.0)
        c_h = cst_ref[sl, :]
        n_h = nst_ref[h:h + 1, :]
        scores = _dot_nt(qh, kh) * w_intra
        num = w_state * _dot(qh, c_h) + _dot(scores, vh)
        den = w_state * jnp.sum(qh * n_h, axis=1, keepdims=True) + \
            jnp.sum(scores, axis=1, keepdims=True)
        outs.append(num / jnp.maximum(jnp.abs(den), jnp.exp(-m_t)))
        m_last = m_t[c - 1:c]
        b_last = bcol[c - 1:c]
        w_s_last = jnp.exp(b_last + mp - m_last)
        kw = kh * jnp.exp(b_last + acol - m_last)
        cst_ref[sl, :] = c_h * w_s_last + _dot(kw.T, vh)
        nst_ref[h:h + 1, :] = n_h * w_s_last + jnp.sum(kw, axis=0, keepdims=True)
        m_new = jnp.where(lane == h, m_last, m_new)
    mst_ref[...] = m_new
    o_c = _head_rmsnorm(jnp.concatenate(outs, axis=1), H_C, DV_C) * gnm_ref[...] * \
        _sigmoid(p_ref[:, 1536:2048])

    rx = p_ref[:, 2048:2560]
    xpad_ref[8:8 + c, :] = rx
    xc = cb_ref[...] + cw_ref[3:4, :] * rx
    for j in range(1, CONV_W):
        xc = xc + cw_ref[3 - j:4 - j, :] * xpad_ref[8 - j:8 - j + c, :]
    conv_ref[...] = xpad_ref[c + 8 - (CONV_W - 1):c + 8, :]
    xpad_ref[0:8, :] = xpad_ref[c:c + 8, :]
    a, u = _rg_gates(xc, wr_ref, br_ref, wi_ref, bi_ref, lam_ref)
    t_idx = _iota((c, D_RG), 0)
    s = 1
    while s < c:
        keep = t_idx >= s
        u = jnp.where(keep, a * pltpu.roll(u, s, 0) + u, u)
        a = jnp.where(keep, a * pltpu.roll(a, s, 0), a)
        s *= 2
    hs = a * hst_ref[...] + u
    hst_ref[...] = hs[c - 1:c]
    o_d = hs * _gelu_tanh(p_ref[:, 2560:3072])

    o_ref[...] = jnp.concatenate([o_c, o_d], axis=1).astype(BF16)


def _odd_prompt(proj, b_ig, b_fg, gn_ml, conv_w, conv_b, w_r, b_r, w_i, b_i, lam, nb, seq):
    c = ODD_CHUNK
    nc = seq // c
    small = lambda shape: pl.BlockSpec(shape, lambda b, n: (0, 0))
    return pl.pallas_call(
        _odd_prompt_kernel,
        grid=(nb, nc),
        in_specs=[pl.BlockSpec((c, D_PROJ), lambda b, n: (b * nc + n, 0)),
                  small((1, LANE)), small((1, LANE)), small((1, D_HALF)),
                  small((CONV_W, D_RG)), small((1, D_RG)),
                  small((D_RG, D_RG)), small((1, D_RG)), small((D_RG, D_RG)), small((1, D_RG)),
                  small((1, D_RG))],
        out_specs=[pl.BlockSpec((c, D_MODEL), lambda b, n: (b * nc + n, 0)),
                   pl.BlockSpec((None, H_C * DK_C, DV_C), lambda b, n: (b, 0, 0)),
                   pl.BlockSpec((None, H_C, DK_C), lambda b, n: (b, 0, 0)),
                   pl.BlockSpec((None, 1, LANE), lambda b, n: (b, 0, 0)),
                   pl.BlockSpec((None, 1, D_RG), lambda b, n: (b, 0, 0)),
                   pl.BlockSpec((None, CONV_W - 1, D_RG), lambda b, n: (b, 0, 0))],
        out_shape=[jax.ShapeDtypeStruct((nb * seq, D_MODEL), BF16),
                   jax.ShapeDtypeStruct((nb, H_C * DK_C, DV_C), F32),
                   jax.ShapeDtypeStruct((nb, H_C, DK_C), F32),
                   jax.ShapeDtypeStruct((nb, 1, LANE), F32),
                   jax.ShapeDtypeStruct((nb, 1, D_RG), F32),
                   jax.ShapeDtypeStruct((nb, CONV_W - 1, D_RG), F32)],
        scratch_shapes=[pltpu.VMEM((c + 8, D_RG), F32)],
        compiler_params=_params("parallel", "arbitrary"),
        name="odd_prompt",
    )(proj, b_ig, b_fg, gn_ml, conv_w, conv_b, w_r, b_r, w_i, b_i, lam)


def _rank1_step(s_in_ref, s_out_ref, o_scr_ref, q_t, k_t, d_t, v, n_heads, dk, dv):
    bb = v.shape[0]
    for j in range(bb):
        vst = jnp.concatenate(
            [jnp.broadcast_to(v[j:j + 1, h * dv:(h + 1) * dv], (dk, dv)) for h in range(n_heads)],
            axis=0)
        s_new = s_in_ref[j] * d_t[:, j:j + 1] + k_t[:, j:j + 1] * vst
        s_out_ref[j] = s_new
        prod = q_t[:, j:j + 1] * s_new
        o_scr_ref[j:j + 1, :] = jnp.concatenate(
            [jnp.sum(prod[h * dk:(h + 1) * dk], axis=0, keepdims=True) for h in range(n_heads)],
            axis=1)


def _even_step_kernel(p_ref, cos_ref, sin_ref, wlr_ref, blr_ref, gng_ref, gnr_ref, sg_in_ref,
                      sr_in_ref, o_ref, sg_out_ref, sr_out_ref, oa_ref, ob_ref):
    bb = STEP_BB
    q = p_ref[:, 0:256] * DK_A ** -0.5
    k = p_ref[:, 256:512]
    v = p_ref[:, 512:1024]
    gg = p_ref[:, 1024:1536]
    d = jnp.exp(_gla_logf(p_ref[:, GATE_OFF:GATE_OFF + LANE], wlr_ref, blr_ref))
    _rank1_step(sg_in_ref, sg_out_ref, oa_ref, q.T, k.T, d.T, v, H_A, DK_A, DV_A)
    o_a = _head_rmsnorm(oa_ref[...], H_A, DV_A) * gng_ref[...] * _silu(gg)

    cos = cos_ref[...]
    sin = sin_ref[...]
    rq = _rope(p_ref[:, 1536:1792], cos, sin, DK_B) * DK_B ** -0.5
    rk = _rope(p_ref[:, 1792:2048], cos, sin, DK_B)
    rv = p_ref[:, 2048:2560]
    rg = p_ref[:, 2560:3072]
    gamma_t = jnp.exp(_head_select(_iota((H_B * DK_B, bb), 0) // DK_B, LOG_GAMMA))
    _rank1_step(sr_in_ref, sr_out_ref, ob_ref, rq.T, rk.T, gamma_t, rv, H_B, DK_B, DV_B)
    o_b = _head_groupnorm(ob_ref[...], H_B, DV_B) * gnr_ref[...] * _silu(rg)
    o_ref[...] = jnp.concatenate([o_a, o_b], axis=1).astype(BF16)


def _even_step(proj, cos, sin, w_lr, b_lr, gn_gla, gn_ret, s_gla, s_ret, idx):
    nb = proj.shape[0]
    bb = STEP_BB
    small = lambda shape: pl.BlockSpec(shape, lambda i: (0, 0))
    state = lambda rows, cols: pl.BlockSpec((bb, rows, cols), lambda i: (i, 0, 0))
    state_in = lambda rows, cols: pl.BlockSpec((None, bb, rows, cols), lambda i: (idx, i, 0, 0))
    return pl.pallas_call(
        _even_step_kernel,
        grid=(nb // bb,),
        in_specs=[pl.BlockSpec((bb, D_PROJ), lambda i: (i, 0)),
                  small((1, H_B * DK_B)), small((1, H_B * DK_B)),
                  small((LANE, H_A * DK_A)), small((1, H_A * DK_A)),
                  small((1, D_HALF)), small((1, D_HALF)),
                  state_in(H_A * DK_A, DV_A), state_in(H_B * DK_B, DV_B)],
        out_specs=[pl.BlockSpec((bb, D_MODEL), lambda i: (i, 0)),
                   state(H_A * DK_A, DV_A), state(H_B * DK_B, DV_B)],
        out_shape=[jax.ShapeDtypeStruct((nb, D_MODEL), BF16),
                   jax.ShapeDtypeStruct(s_gla.shape[1:], F32),
                   jax.ShapeDtypeStruct(s_ret.shape[1:], F32)],
        scratch_shapes=[pltpu.VMEM((bb, D_HALF), F32), pltpu.VMEM((bb, D_HALF), F32)],
        compiler_params=_params("parallel"),
        name="even_step",
    )(proj, cos, sin, w_lr, b_lr, gn_gla, gn_ret, s_gla, s_ret)


def _odd_step_kernel(p_ref, big_ref, bfg_ref, gnm_ref, cw_ref, cb_ref, wr_ref, br_ref, wi_ref,
                     bi_ref, lam_ref, c_in_ref, n_in_ref, m_in_ref, h_in_ref, conv_in_ref,
                     o_ref, c_out_ref, n_out_ref, m_out_ref, h_out_ref, conv_out_ref, num_ref):
    bb = STEP_BB
    gates = p_ref[:, GATE_OFF:GATE_OFF + LANE]
    ig = gates + big_ref[...]
    lf = pltpu.roll(_log_sigmoid(gates + bfg_ref[...]), LANE - H_C, 1)
    m_prev = m_in_ref[...]
    m_new = jnp.maximum(lf + m_prev, ig)
    valid = _iota((bb, LANE), 1) < H_C
    w_s = jnp.where(valid, jnp.exp(lf + m_prev - m_new), 0.0)
    w_i = jnp.where(valid, jnp.exp(ig - m_new), 0.0)
    m_out_ref[...] = jnp.where(valid, m_new, 0.0)
    expand = (_iota((LANE, H_C * DK_C), 0) == _iota((LANE, H_C * DK_C), 1) // DK_C).astype(F32)
    w_s_wide = _dot_f32(w_s, expand)
    w_i_wide = _dot_f32(w_i, expand)
    q = p_ref[:, 0:512] * DK_C ** -0.5
    kw = p_ref[:, 512:1024] * w_i_wide
    v = p_ref[:, 1024:1536]
    _rank1_step(c_in_ref, c_out_ref, num_ref, q.T, kw.T, w_s_wide.T, v, H_C, DK_C, DV_C)
    n_new = n_in_ref[...] * w_s_wide + kw
    n_out_ref[...] = n_new
    num = num_ref[...]
    qn = q * n_new
    outs = []
    for h in range(H_C):
        sl = slice(h * DK_C, (h + 1) * DK_C)
        den = jnp.sum(qn[:, sl], axis=1, keepdims=True)
        outs.append(num[:, sl] / jnp.maximum(jnp.abs(den), jnp.exp(-m_new[:, h:h + 1])))
    o_c = _head_rmsnorm(jnp.concatenate(outs, axis=1), H_C, DV_C) * gnm_ref[...] * \
        _sigmoid(p_ref[:, 1536:2048])

    rx = p_ref[:, 2048:2560]
    xc = cb_ref[...] + cw_ref[3:4, :] * rx
    for j in range(CONV_W - 1):
        xc = xc + cw_ref[j:j + 1, :] * conv_in_ref[j]
    for j in range(CONV_W - 2):
        conv_out_ref[j] = conv_in_ref[j + 1]
    conv_out_ref[CONV_W - 2] = rx
    a, u = _rg_gates(xc, wr_ref, br_ref, wi_ref, bi_ref, lam_ref)
    hs = a * h_in_ref[...] + u
    h_out_ref[...] = hs
    o_d = hs * _gelu_tanh(p_ref[:, 2560:3072])
    o_ref[...] = jnp.concatenate([o_c, o_d], axis=1).astype(BF16)


def _odd_step(proj, b_ig, b_fg, gn_ml, conv_w, conv_b, w_r, b_r, w_i, b_i, lam,
              s_c, s_n, s_m, s_h, s_conv, idx):
    nb = proj.shape[0]
    bb = STEP_BB
    small = lambda shape: pl.BlockSpec(shape, lambda i: (0, 0))
    rows = lambda cols: pl.BlockSpec((bb, cols), lambda i: (i, 0))
    c_spec = pl.BlockSpec((bb, H_C * DK_C, DV_C), lambda i: (i, 0, 0))
    c_in_spec = pl.BlockSpec((None, bb, H_C * DK_C, DV_C), lambda i: (idx, i, 0, 0))
    conv_spec = pl.BlockSpec((CONV_W - 1, bb, D_RG), lambda i: (0, i, 0))
    return pl.pallas_call(
        _odd_step_kernel,
        grid=(nb // bb,),
        in_specs=[rows(D_PROJ),
                  small((1, LANE)), small((1, LANE)), small((1, D_HALF)),
                  small((CONV_W, D_RG)), small((1, D_RG)),
                  small((D_RG, D_RG)), small((1, D_RG)), small((D_RG, D_RG)), small((1, D_RG)),
                  small((1, D_RG)),
                  c_in_spec, rows(H_C * DK_C), rows(LANE), rows(D_RG), conv_spec],
        out_specs=[rows(D_MODEL), c_spec, rows(H_C * DK_C), rows(LANE), rows(D_RG), conv_spec],
        out_shape=[jax.ShapeDtypeStruct((nb, D_MODEL), BF16),
                   jax.ShapeDtypeStruct(s_c.shape[1:], F32),
                   jax.ShapeDtypeStruct(s_n.shape, F32),
                   jax.ShapeDtypeStruct(s_m.shape, F32),
                   jax.ShapeDtypeStruct(s_h.shape, F32),
                   jax.ShapeDtypeStruct(s_conv.shape, F32)],
        scratch_shapes=[pltpu.VMEM((bb, D_HALF), F32)],
        compiler_params=_params("parallel"),
        name="odd_step",
    )(proj, b_ig, b_fg, gn_ml, conv_w, conv_b, w_r, b_r, w_i, b_i, lam, s_c, s_n, s_m, s_h, s_conv)


def _rope_tables(pos):
    half = DK_B // 2
    inv = ROPE_BASE ** (-jnp.arange(half, dtype=F32) / half)
    ang = pos.astype(F32)[:, None] * inv[None, :]
    cos = jnp.tile(jnp.cos(ang), (1, 2 * H_B))
    sin = jnp.tile(jnp.concatenate([-jnp.sin(ang), jnp.sin(ang)], axis=1), (1, H_B))
    return cos, sin


def _reorder_cols(w, gate_lo, gate_hi):
    pad = jnp.zeros(w.shape[:-1] + (LANE - (gate_hi - gate_lo),), w.dtype)
    return jnp.concatenate([w[..., :gate_lo], w[..., gate_hi:], w[..., gate_lo:gate_hi], pad],
                           axis=-1).astype(BF16)


def _pad_lanes(x, offset=0):
    return jnp.pad(x.reshape(1, -1), ((0, 0), (offset, LANE - offset - x.size)))


def _block_diag(w):
    return jax.scipy.linalg.block_diag(*[w[i] for i in range(RG_BLOCKS)]).astype(BF16)


def kernel(x_prompt, x_sample, state_gla, state_ret, state_mlstm_C, state_mlstm_n, state_mlstm_m, state_rglru_h, state_rglru_conv, norm_mix, norm_ffn, norm_final, even_w_in, even_w_lr, even_b_lr, even_gn_gla, even_gn_ret, even_w_out, odd_w_in, ml_b_i, ml_b_f, odd_gn_ml, rg_conv_w, rg_conv_b, rg_w_r, rg_b_r, rg_w_i, rg_b_i, rg_lam, odd_w_out, ffn_w_up, ffn_w_down):
    nb, seq, _ = x_prompt.shape
    ns = x_sample.shape[0]
    row = lambda x: x.reshape(1, -1)

    even_w_in_b = _reorder_cols(even_w_in, 1536, 1536 + GLA_RANK)
    odd_w_in_b = _reorder_cols(odd_w_in, 1536, 1536 + 2 * H_C)
    even_w_out_b = even_w_out.astype(BF16)
    odd_w_out_b = odd_w_out.astype(BF16)
    w_up_b = ffn_w_up.astype(BF16)
    w_down_b = ffn_w_down.astype(BF16)
    s_gla = state_gla.reshape(N_EVEN, ns, H_A * DK_A, DV_A)
    s_ret = state_ret.reshape(N_EVEN, ns, H_B * DK_B, DV_B)
    s_c = state_mlstm_C.reshape(N_ODD, ns, H_C * DK_C, DV_C)
    cos_p, sin_p = _rope_tables(jnp.arange(seq, dtype=jnp.int32))
    cos_s, sin_s = _rope_tables(PAST_LEN + jnp.arange(1, dtype=jnp.int32))

    xp = x_prompt.reshape(nb * seq, D_MODEL)
    xs = x_sample.reshape(ns, D_MODEL)
    st = {k: [] for k in ("gla_p", "ret_p", "c_p", "n_p", "m_p", "h_p", "conv_p",
                          "gla_s", "ret_s", "c_s", "n_s", "m_s", "h_s", "conv_s")}
    for layer in range(DEPTH):
        g_mix = row(norm_mix[layer])
        final = layer == DEPTH - 1
        if layer % 2 == 0:
            e = layer // 2
            w_in, w_out, idx = even_w_in_b, even_w_out_b, e
            w_lr = jnp.pad(even_w_lr[e], ((0, LANE - GLA_RANK), (0, 0)))
            small = (w_lr, row(even_b_lr[e]), row(even_gn_gla[e]), row(even_gn_ret[e]))
            o_p, sg, sr = _even_prompt(_norm_proj(xp, g_mix, w_in, idx, 512), cos_p, sin_p,
                                       *small, nb, seq)
            st["gla_p"].append(sg.reshape(nb, H_A, DK_A, DV_A))
            st["ret_p"].append(sr.reshape(nb, H_B, DK_B, DV_B))
            o_s, sg, sr = _even_step(_norm_proj(xs, g_mix, w_in, idx, ns), cos_s, sin_s, *small,
                                     s_gla, s_ret, e)
            st["gla_s"].append(sg.reshape(ns, H_A, DK_A, DV_A))
            st["ret_s"].append(sr.reshape(ns, H_B, DK_B, DV_B))
        else:
            o = layer // 2
            w_in, w_out, idx = odd_w_in_b, odd_w_out_b, o
            small = (_pad_lanes(ml_b_i[o]), _pad_lanes(ml_b_f[o], H_C), row(odd_gn_ml[o]),
                     rg_conv_w[o], row(rg_conv_b[o]), _block_diag(rg_w_r[o]), row(rg_b_r[o]),
                     _block_diag(rg_w_i[o]), row(rg_b_i[o]), row(rg_lam[o]))
            o_p, c_, n_, m_, h_, cv = _odd_prompt(_norm_proj(xp, g_mix, w_in, idx, 512), *small,
                                                  nb, seq)
            st["c_p"].append(c_.reshape(nb, H_C, DK_C, DV_C))
            st["n_p"].append(n_)
            st["m_p"].append(m_[:, 0, :H_C])
            st["h_p"].append(h_[:, 0, :])
            st["conv_p"].append(cv)
            o_s, c_, n_, m_, h_, cv = _odd_step(
                _norm_proj(xs, g_mix, w_in, idx, ns), *small,
                s_c,
                state_mlstm_n[o].reshape(ns, H_C * DK_C),
                jnp.pad(state_mlstm_m[o], ((0, 0), (0, LANE - H_C))),
                state_rglru_h[o],
                jnp.swapaxes(state_rglru_conv[o], 0, 1), o)
            st["c_s"].append(c_.reshape(ns, H_C, DK_C, DV_C))
            st["n_s"].append(n_.reshape(ns, H_C, DK_C))
            st["m_s"].append(m_[:, :H_C])
            st["h_s"].append(h_)
            st["conv_s"].append(jnp.swapaxes(cv, 0, 1))
        ffn = (row(norm_ffn[layer]), w_up_b, w_down_b, layer, row(norm_final))
        xp = _out_ffn(xp, o_p, w_out, idx, *ffn, 512, final)
        xs = _out_ffn(xs, o_s, w_out, idx, *ffn, ns, final)

    stack = lambda name: jnp.stack(st[name])
    return (xp.reshape(nb, seq, D_MODEL), xs.reshape(ns, 1, D_MODEL),
            stack("gla_p"), stack("ret_p"), stack("c_p"), stack("n_p"), stack("m_p"),
            stack("h_p"), stack("conv_p"),
            stack("gla_s"), stack("ret_s"), stack("c_s"), stack("n_s"), stack("m_s"),
            stack("h_s"), stack("conv_s"))
```

```python
import functools

import numpy as np
import jax
import jax.numpy as jnp
from jax import lax
from jax.experimental import pallas as pl
from jax.experimental.pallas import tpu as pltpu

F32 = jnp.float32
BF16 = jnp.bfloat16

D_MODEL = 1024
DEPTH = 4
PAST_LEN = 16384
N_EVEN = (DEPTH + 1) // 2
N_ODD = DEPTH // 2
D_HALF = D_MODEL // 2
H_A = 4
DV_A = D_HALF // H_A
DK_A = DV_A // 2
GLA_RANK = 16
GLA_TAU = 16.0
H_B = 4
DV_B = D_HALF // H_B
DK_B = DV_B // 2
ROPE_BASE = 10000.0
H_C = 4
DK_C = D_HALF // H_C
DV_C = D_HALF // H_C
D_RG = D_HALF
RG_BLOCKS = 8
RG_BW = D_RG // RG_BLOCKS
RG_C = 8.0
CONV_W = 4
D_FF = (8 * D_MODEL + 3 * 256 - 1) // (3 * 256) * 256
EPS = 1e-6

LANE = 128
SUBLANE = 8
D_PROJ = 3072 + LANE
GATE_OFF = 3072
EVEN_CHUNK = 64
GLA_SUB = 16
ODD_CHUNK = 128
PROMPT_SEQS = 4
STEP_BB = 8
VMEM_LIMIT = 56 * 1024 * 1024

LOG_GAMMA = [float(np.log1p(-np.exp2(-5.0 - h))) for h in range(H_B)]


def _dot(a, b):
    return jnp.dot(a.astype(BF16), b.astype(BF16), preferred_element_type=F32)


def _dot_nt(a, b):
    return lax.dot_general(a.astype(BF16), b.astype(BF16), (((1,), (1,)), ((), ())),
                           preferred_element_type=F32)


def _dot_f32(a, b):
    return jnp.dot(a, b, preferred_element_type=F32, precision=lax.Precision.HIGHEST)


def _rms(x, g):
    return x * lax.rsqrt(jnp.mean(x * x, axis=-1, keepdims=True) + EPS) * g


def _softplus(x):
    return jnp.maximum(x, 0.0) + jnp.log1p(jnp.exp(-jnp.abs(x)))


def _log_sigmoid(x):
    return -_softplus(-x)


def _sigmoid(x):
    return 0.5 * jnp.tanh(0.5 * x) + 0.5


def _silu(x):
    return x * _sigmoid(x)


def _gelu_tanh(x):
    return 0.5 * x * (1.0 + jnp.tanh(0.7978845608028654 * (x + 0.044715 * x * x * x)))


def _iota(shape, dim):
    return lax.broadcasted_iota(jnp.int32, shape, dim)


def _tri(c):
    return (_iota((c, c), 0) >= _iota((c, c), 1)).astype(F32)


def _stack_heads(x, n_heads, dk):
    head = _iota(x.shape, 1) // dk
    return jnp.concatenate([jnp.where(head == h, x, 0.0) for h in range(n_heads)], axis=0)


def _diag_lanes(y, n_heads, rows, cols):
    return jnp.concatenate(
        [y[h * rows:(h + 1) * rows, h * cols:(h + 1) * cols] for h in range(n_heads)], axis=1)


def _diag_rows(y, n_heads, rows, cols):
    return jnp.concatenate(
        [y[h * rows:(h + 1) * rows, h * cols:(h + 1) * cols] for h in range(n_heads)], axis=0)


def _head_select(idx, vals):
    out = jnp.full(idx.shape, vals[-1], F32)
    for h in range(len(vals) - 2, -1, -1):
        out = jnp.where(idx == h, vals[h], out)
    return out


def _swap_halves(x, dk):
    half = dk // 2
    parts = []
    for c in range(x.shape[1] // LANE):
        xs = x[:, c * LANE:(c + 1) * LANE]
        lo = _iota(xs.shape, 1) % dk < half
        parts.append(jnp.where(lo, pltpu.roll(xs, LANE - half, 1), pltpu.roll(xs, half, 1)))
    return jnp.concatenate(parts, axis=1)


def _rope(x, cos, sin_signed, dk):
    return x * cos + _swap_halves(x, dk) * sin_signed


def _head_rmsnorm(o, n_heads, d):
    parts = []
    for h in range(n_heads):
        oh = o[:, h * d:(h + 1) * d]
        parts.append(oh * lax.rsqrt(jnp.mean(oh * oh, axis=-1, keepdims=True) + EPS))
    return jnp.concatenate(parts, axis=1)


def _head_groupnorm(o, n_heads, d):
    parts = []
    for h in range(n_heads):
        oh = o[:, h * d:(h + 1) * d]
        c = oh - jnp.mean(oh, axis=-1, keepdims=True)
        parts.append(c * lax.rsqrt(jnp.mean(c * c, axis=-1, keepdims=True) + EPS))
    return jnp.concatenate(parts, axis=1)


def _params(*sem):
    return pltpu.CompilerParams(dimension_semantics=sem, vmem_limit_bytes=VMEM_LIMIT)


def _norm_proj_kernel(x_ref, g_ref, w_ref, o_ref):
    o_ref[...] = _dot(_rms(x_ref[...], g_ref[...]), w_ref[...])


def _resident(shape, index):
    return pl.BlockSpec(shape, lambda i: index, pipeline_mode=pl.Buffered(1))


def _norm_proj(x, g, w_stack, idx, tm):
    m = x.shape[0]
    n = w_stack.shape[2]
    return pl.pallas_call(
        _norm_proj_kernel,
        grid=(m // tm,),
        in_specs=[pl.BlockSpec((tm, D_MODEL), lambda i: (i, 0)),
                  _resident((1, D_MODEL), (0, 0)),
                  _resident((None, D_MODEL, n), (idx, 0, 0))],
        out_specs=pl.BlockSpec((tm, n), lambda i: (i, 0)),
        out_shape=jax.ShapeDtypeStruct((m, n), F32),
        compiler_params=_params("parallel"),
        name="norm_proj",
    )(x, g, w_stack)


def _out_ffn_kernel(x_ref, o_ref, wo_ref, g_ref, wg_ref, wu_ref, wd_ref, gf_ref, y_ref, *, final):
    xn = x_ref[...] + _dot(o_ref[...], wo_ref[...])
    hn = _rms(xn, g_ref[...]).astype(BF16)
    gate = _dot(hn, wg_ref[...])
    up = _dot(hn, wu_ref[...])
    y = xn + _dot(_silu(gate) * up, wd_ref[...])
    if final:
        y = _rms(y, gf_ref[...])
    y_ref[...] = y


def _out_ffn(x, o, w_out_stack, idx, g_ffn, w_up_stack, w_down_stack, layer, g_final, tm, final):
    m = x.shape[0]
    return pl.pallas_call(
        functools.partial(_out_ffn_kernel, final=final),
        grid=(m // tm,),
        in_specs=[pl.BlockSpec((tm, D_MODEL), lambda i: (i, 0)),
                  pl.BlockSpec((tm, D_MODEL), lambda i: (i, 0)),
                  _resident((None, D_MODEL, D_MODEL), (idx, 0, 0)),
                  _resident((1, D_MODEL), (0, 0)),
                  _resident((None, D_MODEL, D_FF), (layer, 0, 0)),
                  _resident((None, D_MODEL, D_FF), (layer, 0, 1)),
                  _resident((None, D_FF, D_MODEL), (layer, 0, 0)),
                  _resident((1, D_MODEL), (0, 0))],
        out_specs=pl.BlockSpec((tm, D_MODEL), lambda i: (i, 0)),
        out_shape=jax.ShapeDtypeStruct((m, D_MODEL), F32),
        compiler_params=_params("parallel"),
        name="out_ffn",
    )(x, o, w_out_stack, g_ffn, w_up_stack, w_up_stack, w_down_stack, g_final)


def _gla_logf(glr, wlr_ref, blr_ref):
    return _log_sigmoid(_dot_f32(glr, wlr_ref[...]) + blr_ref[...]) * (1.0 / GLA_TAU)


def _even_prompt_kernel(p_ref, cos_ref, sin_ref, wlr_ref, blr_ref, gng_ref, gnr_ref,
                        o_ref, sg_ref, sr_ref):
    @pl.when(pl.program_id(1) == 0)
    def _():
        sg_ref[...] = jnp.zeros_like(sg_ref)
        sr_ref[...] = jnp.zeros_like(sr_ref)

    for s in range(PROMPT_SEQS):
        _even_chunk(p_ref.at[s], cos_ref, sin_ref, wlr_ref, blr_ref, gng_ref, gnr_ref,
                    o_ref.at[s], sg_ref.at[s], sr_ref.at[s])


def _even_chunk(p_ref, cos_ref, sin_ref, wlr_ref, blr_ref, gng_ref, gnr_ref, o_ref, sg_ref, sr_ref):
    c = EVEN_CHUNK
    r_sub = GLA_SUB

    q = p_ref[:, 0:256] * DK_A ** -0.5
    k = p_ref[:, 256:512]
    v = p_ref[:, 512:1024]
    gg = p_ref[:, 1024:1536]
    logf = _gla_logf(p_ref[:, GATE_OFF:GATE_OFF + LANE], wlr_ref, blr_ref)
    b = _dot_f32(_tri(c), logf)
    s_gla = sg_ref[...]
    inter = _dot(_stack_heads(q * jnp.exp(b), H_A, DK_A), s_gla)
    o_a = jnp.concatenate([inter[h * c:(h + 1) * c] for h in range(H_A)], axis=1)

    ind = (_iota((H_A * DK_A, H_A * DV_A), 0) // DK_A ==
           _iota((H_A * DK_A, H_A * DV_A), 1) // DV_A).astype(BF16)
    row = _iota((r_sub, H_A * DK_A), 0)
    blocks = []
    for i in range(c // r_sub):
        lo = i * r_sub
        qb = q[lo:lo + r_sub]
        bb = b[lo:lo + r_sub]
        ws = []
        for s in range(r_sub):
            e = jnp.exp(jnp.minimum(bb - b[lo + s:lo + s + 1], 0.0))
            ws.append(jnp.where(row >= s, qb * k[lo + s:lo + s + 1] * e, 0.0))
        rep = _dot(jnp.concatenate(ws, axis=0), ind)
        ob = rep[0:r_sub] * v[lo:lo + 1]
        for s in range(1, r_sub):
            ob = ob + rep[s * r_sub:(s + 1) * r_sub] * v[lo + s:lo + s + 1]
        if i > 0:
            ref_b = b[lo - 1:lo]
            qi = qb * jnp.exp(bb - ref_b)
            ki = k[0:lo] * jnp.exp(ref_b - b[0:lo])
            sc = _dot_nt(_stack_heads(qi, H_A, DK_A), ki)
            ob = ob + _diag_lanes(_dot(sc, v[0:lo]), H_A, r_sub, DV_A)
        blocks.append(ob)
    o_a = o_a + jnp.concatenate(blocks, axis=0)
    o_a = _head_rmsnorm(o_a, H_A, DV_A) * gng_ref[...] * _silu(gg)

    b_t = b.T
    b_last = b_t[:, c - 1:c]
    kd_t = k.T * jnp.exp(b_last - b_t)
    upd = _diag_rows(_dot(kd_t, v), H_A, DK_A, DV_A)
    sg_ref[...] = s_gla * jnp.exp(b_last) + upd

    cos = cos_ref[...]
    sin = sin_ref[...]
    rq = _rope(p_ref[:, 1536:1792], cos, sin, DK_B) * DK_B ** -0.5
    rk = _rope(p_ref[:, 1792:2048], cos, sin, DK_B)
    rv = p_ref[:, 2048:2560]
    rg = p_ref[:, 2560:3072]
    s_ret = sr_ref[...]
    t_row = _iota((H_B * c, c), 0)
    lg_row = _head_select(t_row // c, LOG_GAMMA)
    t_loc = (t_row % c).astype(F32)
    s_loc = _iota((H_B * c, c), 1).astype(F32)
    decay = jnp.where(t_loc >= s_loc, jnp.exp(jnp.maximum(t_loc - s_loc, 0.0) * lg_row), 0.0)
    q_dec = jnp.exp((t_loc[:, 0:1] + 1.0) * lg_row[:, 0:1])
    qs = _stack_heads(rq, H_B, DK_B)
    scores = _dot_nt(qs, rk) * decay
    o_b = _diag_lanes(_dot(scores, rv), H_B, c, DV_B)
    inter_b = _dot(qs * q_dec, s_ret)
    o_b = o_b + jnp.concatenate([inter_b[h * c:(h + 1) * c] for h in range(H_B)], axis=1)
    o_b = _head_groupnorm(o_b, H_B, DV_B) * gnr_ref[...] * _silu(rg)

    lg_lane = _head_select(_iota((c, H_B * DK_B), 1) // DK_B, LOG_GAMMA)
    k_dec = jnp.exp((c - 1.0 - _iota((c, H_B * DK_B), 0).astype(F32)) * lg_lane)
    upd_b = _diag_rows(_dot((rk * k_dec).T, rv), H_B, DK_B, DV_B)
    g_chunk = jnp.exp(c * _head_select(_iota((H_B * DK_B, DV_B), 0) // DK_B, LOG_GAMMA))
    sr_ref[...] = s_ret * g_chunk + upd_b

    o_ref[...] = jnp.concatenate([o_a, o_b], axis=1).astype(BF16)


def _even_prompt(proj, cos, sin, w_lr, b_lr, gn_gla, gn_ret, nb, seq):
    c = EVEN_CHUNK
    ps = PROMPT_SEQS
    small = lambda shape: pl.BlockSpec(shape, lambda b, n: (0, 0))
    per_seq = lambda rows, cols: pl.BlockSpec((ps, rows, cols), lambda b, n: (b, 0, 0))
    o, sg, sr = pl.pallas_call(
        _even_prompt_kernel,
        grid=(nb // ps, seq // c),
        in_specs=[pl.BlockSpec((ps, c, D_PROJ), lambda b, n: (b, n, 0)),
                  pl.BlockSpec((c, H_B * DK_B), lambda b, n: (n, 0)),
                  pl.BlockSpec((c, H_B * DK_B), lambda b, n: (n, 0)),
                  small((LANE, H_A * DK_A)), small((1, H_A * DK_A)),
                  small((1, D_HALF)), small((1, D_HALF))],
        out_specs=[pl.BlockSpec((ps, c, D_MODEL), lambda b, n: (b, n, 0)),
                   per_seq(H_A * DK_A, DV_A), per_seq(H_B * DK_B, DV_B)],
        out_shape=[jax.ShapeDtypeStruct((nb, seq, D_MODEL), BF16),
                   jax.ShapeDtypeStruct((nb, H_A * DK_A, DV_A), F32),
                   jax.ShapeDtypeStruct((nb, H_B * DK_B, DV_B), F32)],
        compiler_params=_params("parallel", "arbitrary"),
        name="even_prompt",
    )(proj.reshape(nb, seq, D_PROJ), cos, sin, w_lr, b_lr, gn_gla, gn_ret)
    return o.reshape(nb * seq, D_MODEL), sg, sr


def _rg_gates(xc, wr_ref, br_ref, wi_ref, bi_ref, lam_ref):
    r = _sigmoid(_dot(xc, wr_ref[...]) + br_ref[...])
    i = _sigmoid(_dot(xc, wi_ref[...]) + bi_ref[...])
    log_a = -RG_C * r * _softplus(-lam_ref[...])
    a = jnp.exp(log_a)
    u = jnp.sqrt(jnp.tanh(-log_a) * (a * a + 1.0)) * (i * xc)
    return a, u


def _odd_prompt_kernel(p_ref, big_ref, bfg_ref, gnm_ref, cw_ref, cb_ref, wr_ref, br_ref, wi_ref,
                       bi_ref, lam_ref, o_ref, cst_ref, nst_ref, mst_ref, hst_ref, conv_ref,
                       xpad_ref):
    @pl.when(pl.program_id(1) == 0)
    def _():
        cst_ref[...] = jnp.zeros_like(cst_ref)
        nst_ref[...] = jnp.zeros_like(nst_ref)
        mst_ref[...] = jnp.zeros_like(mst_ref)
        hst_ref[...] = jnp.zeros_like(hst_ref)
        xpad_ref[:, 0:8, :] = jnp.zeros((PROMPT_SEQS, 8, D_RG), F32)

    for s in range(PROMPT_SEQS):
        _odd_chunk(p_ref.at[s], big_ref, bfg_ref, gnm_ref, cw_ref, cb_ref, wr_ref, br_ref, wi_ref,
                   bi_ref, lam_ref, o_ref.at[s], cst_ref.at[s], nst_ref.at[s], mst_ref.at[s],
                   hst_ref.at[s], conv_ref.at[s], xpad_ref.at[s])


def _odd_chunk(p_ref, big_ref, bfg_ref, gnm_ref, cw_ref, cb_ref, wr_ref, br_ref, wi_ref,
               bi_ref, lam_ref, o_ref, cst_ref, nst_ref, mst_ref, hst_ref, conv_ref, xpad_ref):
    c = ODD_CHUNK

    gates = p_ref[:, GATE_OFF:GATE_OFF + LANE]
    ig = gates + big_ref[...]
    lf = _log_sigmoid(gates + bfg_ref[...])
    b_all = pltpu.roll(_dot_f32(_tri(c), lf), LANE - H_C, 1)
    a_all = ig - b_all
    a_rows = a_all.T
    causal = _iota((c, c), 0) >= _iota((c, c), 1)
    m_prev = mst_ref[...]
    m_new = m_prev
    lane = _iota((1, LANE), 1)
    outs = []
    for h in range(H_C):
        sl = slice(h * DK_C, (h + 1) * DK_C)
        qh = p_ref[:, sl] * DK_C ** -0.5
        kh = p_ref[:, 512 + h * DK_C:512 + (h + 1) * DK_C]
        vh = p_ref[:, 1024 + h * DV_C:1024 + (h + 1) * DV_C]
        bcol = b_all[:, h:h + 1]
        acol = a_all[:, h:h + 1]
        arow = a_rows[h:h + 1, :]
        mp = m_prev[:, h:h + 1]
        cmax = jnp.max(jnp.where(causal, arow, -jnp.inf), axis=1, keepdims=True)
        m_t = jnp.maximum(bcol + mp, bcol + cmax)
        w_state = jnp.exp(bcol + mp - m_t)
        w_intra = jnp.where(causal, jnp.exp(jnp.minimum(bcol + arow - m_t, 0.0)), 0.0)
        c_h = cst_ref[sl, :]
        n_h = nst_ref[h:h + 1, :]
        scores = _dot_nt(qh, kh) * w_intra
        num = w_state * _dot(qh, c_h) + _dot(scores, vh)
        den = w_state * jnp.sum(qh * n_h, axis=1, keepdims=True) + \
            jnp.sum(scores, axis=1, keepdims=True)
        outs.append(num / jnp.maximum(jnp.abs(den), jnp.exp(-m_t)))
        m_last = m_t[c - 1:c]
        b_last = bcol[c - 1:c]
        w_s_last = jnp.exp(b_last + mp - m_last)
        kw = kh * jnp.exp(b_last + acol - m_last)
        cst_ref[sl, :] = c_h * w_s_last + _dot(kw.T, vh)
        nst_ref[h:h + 1, :] = n_h * w_s_last + jnp.sum(kw, axis=0, keepdims=True)
        m_new = jnp.where(lane == h, m_last, m_new)
    mst_ref[...] = m_new
    o_c = _head_rmsnorm(jnp.concatenate(outs, axis=1), H_C, DV_C) * gnm_ref[...] * \
        _sigmoid(p_ref[:, 1536:2048])

    rx = p_ref[:, 2048:2560]
    xpad_ref[8:8 + c, :] = rx
    xc = cb_ref[...] + cw_ref[3:4, :] * rx
    for j in range(1, CONV_W):
        xc = xc + cw_ref[3 - j:4 - j, :] * xpad_ref[8 - j:8 - j + c, :]
    conv_ref[...] = xpad_ref[c + 8 - (CONV_W - 1):c + 8, :]
    xpad_ref[0:8, :] = xpad_ref[c:c + 8, :]
    a, u = _rg_gates(xc, wr_ref, br_ref, wi_ref, bi_ref, lam_ref)
    a = a.reshape(c // SUBLANE, SUBLANE, D_RG)
    u = u.reshape(c // SUBLANE, SUBLANE, D_RG)
    t_idx = _iota(a.shape, 1)
    s = 1
    while s < SUBLANE:
        keep = t_idx >= s
        u = jnp.where(keep, a * pltpu.roll(u, s, 1) + u, u)
        a = jnp.where(keep, a * pltpu.roll(a, s, 1), a)
        s *= 2
    carry = hst_ref[...]
    groups = []
    for g in range(c // SUBLANE):
        hg = a[g] * carry + u[g]
        carry = hg[SUBLANE - 1:SUBLANE]
        groups.append(hg)
    hs = jnp.concatenate(groups, axis=0)
    hst_ref[...] = carry
    o_d = hs * _gelu_tanh(p_ref[:, 2560:3072])

    o_ref[...] = jnp.concatenate([o_c, o_d], axis=1).astype(BF16)


def _odd_prompt(proj, b_ig, b_fg, gn_ml, conv_w, conv_b, w_r, b_r, w_i, b_i, lam, nb, seq):
    c = ODD_CHUNK
    ps = PROMPT_SEQS
    small = lambda shape: pl.BlockSpec(shape, lambda b, n: (0, 0))
    per_seq = lambda rows, cols: pl.BlockSpec((ps, rows, cols), lambda b, n: (b, 0, 0))
    outs = pl.pallas_call(
        _odd_prompt_kernel,
        grid=(nb // ps, seq // c),
        in_specs=[pl.BlockSpec((ps, c, D_PROJ), lambda b, n: (b, n, 0)),
                  small((1, LANE)), small((1, LANE)), small((1, D_HALF)),
                  small((CONV_W, D_RG)), small((1, D_RG)),
                  small((D_RG, D_RG)), small((1, D_RG)), small((D_RG, D_RG)), small((1, D_RG)),
                  small((1, D_RG))],
        out_specs=[pl.BlockSpec((ps, c, D_MODEL), lambda b, n: (b, n, 0)),
                   per_seq(H_C * DK_C, DV_C), per_seq(H_C, DK_C), per_seq(1, LANE),
                   per_seq(1, D_RG), per_seq(CONV_W - 1, D_RG)],
        out_shape=[jax.ShapeDtypeStruct((nb, seq, D_MODEL), BF16),
                   jax.ShapeDtypeStruct((nb, H_C * DK_C, DV_C), F32),
                   jax.ShapeDtypeStruct((nb, H_C, DK_C), F32),
                   jax.ShapeDtypeStruct((nb, 1, LANE), F32),
                   jax.ShapeDtypeStruct((nb, 1, D_RG), F32),
                   jax.ShapeDtypeStruct((nb, CONV_W - 1, D_RG), F32)],
        scratch_shapes=[pltpu.VMEM((ps, c + 8, D_RG), F32)],
        compiler_params=_params("parallel", "arbitrary"),
        name="odd_prompt",
    )(proj.reshape(nb, seq, D_PROJ), b_ig, b_fg, gn_ml, conv_w, conv_b, w_r, b_r, w_i, b_i, lam)
    return (outs[0].reshape(nb * seq, D_MODEL),) + tuple(outs[1:])


def _rank1_step(s_in_ref, s_out_ref, o_scr_ref, q_t, k_t, d_t, v, n_heads, dk, dv):
    bb = v.shape[0]
    for j in range(bb):
        vst = jnp.concatenate(
            [jnp.broadcast_to(v[j:j + 1, h * dv:(h + 1) * dv], (dk, dv)) for h in range(n_heads)],
            axis=0)
        s_new = s_in_ref[j] * d_t[:, j:j + 1] + k_t[:, j:j + 1] * vst
        s_out_ref[j] = s_new
        prod = q_t[:, j:j + 1] * s_new
        o_scr_ref[j:j + 1, :] = jnp.concatenate(
            [jnp.sum(prod[h * dk:(h + 1) * dk], axis=0, keepdims=True) for h in range(n_heads)],
            axis=1)


def _even_step_kernel(p_ref, cos_ref, sin_ref, wlr_ref, blr_ref, gng_ref, gnr_ref, sg_in_ref,
                      sr_in_ref, o_ref, sg_out_ref, sr_out_ref, oa_ref, ob_ref):
    bb = STEP_BB
    q = p_ref[:, 0:256] * DK_A ** -0.5
    k = p_ref[:, 256:512]
    v = p_ref[:, 512:1024]
    gg = p_ref[:, 1024:1536]
    d = jnp.exp(_gla_logf(p_ref[:, GATE_OFF:GATE_OFF + LANE], wlr_ref, blr_ref))
    _rank1_step(sg_in_ref, sg_out_ref, oa_ref, q.T, k.T, d.T, v, H_A, DK_A, DV_A)
    o_a = _head_rmsnorm(oa_ref[...], H_A, DV_A) * gng_ref[...] * _silu(gg)

    cos = cos_ref[...]
    sin = sin_ref[...]
    rq = _rope(p_ref[:, 1536:1792], cos, sin, DK_B) * DK_B ** -0.5
    rk = _rope(p_ref[:, 1792:2048], cos, sin, DK_B)
    rv = p_ref[:, 2048:2560]
    rg = p_ref[:, 2560:3072]
    gamma_t = jnp.exp(_head_select(_iota((H_B * DK_B, bb), 0) // DK_B, LOG_GAMMA))
    _rank1_step(sr_in_ref, sr_out_ref, ob_ref, rq.T, rk.T, gamma_t, rv, H_B, DK_B, DV_B)
    o_b = _head_groupnorm(ob_ref[...], H_B, DV_B) * gnr_ref[...] * _silu(rg)
    o_ref[...] = jnp.concatenate([o_a, o_b], axis=1).astype(BF16)


def _even_step(proj, cos, sin, w_lr, b_lr, gn_gla, gn_ret, s_gla, s_ret, idx):
    nb = proj.shape[0]
    bb = STEP_BB
    small = lambda shape: pl.BlockSpec(shape, lambda i: (0, 0))
    state = lambda rows, cols: pl.BlockSpec((bb, rows, cols), lambda i: (i, 0, 0))
    state_in = lambda rows, cols: pl.BlockSpec((None, bb, rows, cols), lambda i: (idx, i, 0, 0))
    return pl.pallas_call(
        _even_step_kernel,
        grid=(nb // bb,),
        in_specs=[pl.BlockSpec((bb, D_PROJ), lambda i: (i, 0)),
                  small((1, H_B * DK_B)), small((1, H_B * DK_B)),
                  small((LANE, H_A * DK_A)), small((1, H_A * DK_A)),
                  small((1, D_HALF)), small((1, D_HALF)),
                  state_in(H_A * DK_A, DV_A), state_in(H_B * DK_B, DV_B)],
        out_specs=[pl.BlockSpec((bb, D_MODEL), lambda i: (i, 0)),
                   state(H_A * DK_A, DV_A), state(H_B * DK_B, DV_B)],
        out_shape=[jax.ShapeDtypeStruct((nb, D_MODEL), BF16),
                   jax.ShapeDtypeStruct(s_gla.shape[1:], F32),
                   jax.ShapeDtypeStruct(s_ret.shape[1:], F32)],
        scratch_shapes=[pltpu.VMEM((bb, D_HALF), F32), pltpu.VMEM((bb, D_HALF), F32)],
        compiler_params=_params("parallel"),
        name="even_step",
    )(proj, cos, sin, w_lr, b_lr, gn_gla, gn_ret, s_gla, s_ret)


def _odd_step_kernel(p_ref, big_ref, bfg_ref, gnm_ref, cw_ref, cb_ref, wr_ref, br_ref, wi_ref,
                     bi_ref, lam_ref, c_in_ref, n_in_ref, m_in_ref, h_in_ref, conv_in_ref,
                     o_ref, c_out_ref, n_out_ref, m_out_ref, h_out_ref, conv_out_ref, num_ref):
    bb = STEP_BB
    gates = p_ref[:, GATE_OFF:GATE_OFF + LANE]
    ig = gates + big_ref[...]
    lf = pltpu.roll(_log_sigmoid(gates + bfg_ref[...]), LANE - H_C, 1)
    m_prev = m_in_ref[...]
    m_new = jnp.maximum(lf + m_prev, ig)
    valid = _iota((bb, LANE), 1) < H_C
    w_s = jnp.where(valid, jnp.exp(lf + m_prev - m_new), 0.0)
    w_i = jnp.where(valid, jnp.exp(ig - m_new), 0.0)
    m_out_ref[...] = jnp.where(valid, m_new, 0.0)
    expand = (_iota((LANE, H_C * DK_C), 0) == _iota((LANE, H_C * DK_C), 1) // DK_C).astype(F32)
    w_s_wide = _dot_f32(w_s, expand)
    w_i_wide = _dot_f32(w_i, expand)
    q = p_ref[:, 0:512] * DK_C ** -0.5
    kw = p_ref[:, 512:1024] * w_i_wide
    v = p_ref[:, 1024:1536]
    _rank1_step(c_in_ref, c_out_ref, num_ref, q.T, kw.T, w_s_wide.T, v, H_C, DK_C, DV_C)
    n_new = n_in_ref[...] * w_s_wide + kw
    n_out_ref[...] = n_new
    num = num_ref[...]
    qn = q * n_new
    outs = []
    for h in range(H_C):
        sl = slice(h * DK_C, (h + 1) * DK_C)
        den = jnp.sum(qn[:, sl], axis=1, keepdims=True)
        outs.append(num[:, sl] / jnp.maximum(jnp.abs(den), jnp.exp(-m_new[:, h:h + 1])))
    o_c = _head_rmsnorm(jnp.concatenate(outs, axis=1), H_C, DV_C) * gnm_ref[...] * \
        _sigmoid(p_ref[:, 1536:2048])

    rx = p_ref[:, 2048:2560]
    xc = cb_ref[...] + cw_ref[3:4, :] * rx
    for j in range(CONV_W - 1):
        xc = xc + cw_ref[j:j + 1, :] * conv_in_ref[j]
    for j in range(CONV_W - 2):
        conv_out_ref[j] = conv_in_ref[j + 1]
    conv_out_ref[CONV_W - 2] = rx
    a, u = _rg_gates(xc, wr_ref, br_ref, wi_ref, bi_ref, lam_ref)
    hs = a * h_in_ref[...] + u
    h_out_ref[...] = hs
    o_d = hs * _gelu_tanh(p_ref[:, 2560:3072])
    o_ref[...] = jnp.concatenate([o_c, o_d], axis=1).astype(BF16)


def _odd_step(proj, b_ig, b_fg, gn_ml, conv_w, conv_b, w_r, b_r, w_i, b_i, lam,
              s_c, s_n, s_m, s_h, s_conv, idx):
    nb = proj.shape[0]
    bb = STEP_BB
    small = lambda shape: pl.BlockSpec(shape, lambda i: (0, 0))
    rows = lambda cols: pl.BlockSpec((bb, cols), lambda i: (i, 0))
    c_spec = pl.BlockSpec((bb, H_C * DK_C, DV_C), lambda i: (i, 0, 0))
    c_in_spec = pl.BlockSpec((None, bb, H_C * DK_C, DV_C), lambda i: (idx, i, 0, 0))
    conv_spec = pl.BlockSpec((CONV_W - 1, bb, D_RG), lambda i: (0, i, 0))
    return pl.pallas_call(
        _odd_step_kernel,
        grid=(nb // bb,),
        in_specs=[rows(D_PROJ),
                  small((1, LANE)), small((1, LANE)), small((1, D_HALF)),
                  small((CONV_W, D_RG)), small((1, D_RG)),
                  small((D_RG, D_RG)), small((1, D_RG)), small((D_RG, D_RG)), small((1, D_RG)),
                  small((1, D_RG)),
                  c_in_spec, rows(H_C * DK_C), rows(LANE), rows(D_RG), conv_spec],
        out_specs=[rows(D_MODEL), c_spec, rows(H_C * DK_C), rows(LANE), rows(D_RG), conv_spec],
        out_shape=[jax.ShapeDtypeStruct((nb, D_MODEL), BF16),
                   jax.ShapeDtypeStruct(s_c.shape[1:], F32),
                   jax.ShapeDtypeStruct(s_n.shape, F32),
                   jax.ShapeDtypeStruct(s_m.shape, F32),
                   jax.ShapeDtypeStruct(s_h.shape, F32),
                   jax.ShapeDtypeStruct(s_conv.shape, F32)],
        scratch_shapes=[pltpu.VMEM((bb, D_HALF), F32)],
        compiler_params=_params("parallel"),
        name="odd_step",
    )(proj, b_ig, b_fg, gn_ml, conv_w, conv_b, w_r, b_r, w_i, b_i, lam, s_c, s_n, s_m, s_h, s_conv)


def _rope_tables(pos):
    half = DK_B // 2
    inv = ROPE_BASE ** (-jnp.arange(half, dtype=F32) / half)
    ang = pos.astype(F32)[:, None] * inv[None, :]
    cos = jnp.tile(jnp.cos(ang), (1, 2 * H_B))
    sin = jnp.tile(jnp.concatenate([-jnp.sin(ang), jnp.sin(ang)], axis=1), (1, H_B))
    return cos, sin


def _reorder_cols(w, gate_lo, gate_hi):
    pad = jnp.zeros(w.shape[:-1] + (LANE - (gate_hi - gate_lo),), w.dtype)
    return jnp.concatenate([w[..., :gate_lo], w[..., gate_hi:], w[..., gate_lo:gate_hi], pad],
                           axis=-1).astype(BF16)


def _pad_lanes(x, offset=0):
    return jnp.pad(x.reshape(1, -1), ((0, 0), (offset, LANE - offset - x.size)))


def _block_diag(w):
    return jax.scipy.linalg.block_diag(*[w[i] for i in range(RG_BLOCKS)]).astype(BF16)


def kernel(x_prompt, x_sample, state_gla, state_ret, state_mlstm_C, state_mlstm_n, state_mlstm_m, state_rglru_h, state_rglru_conv, norm_mix, norm_ffn, norm_final, even_w_in, even_w_lr, even_b_lr, even_gn_gla, even_gn_ret, even_w_out, odd_w_in, ml_b_i, ml_b_f, odd_gn_ml, rg_conv_w, rg_conv_b, rg_w_r, rg_b_r, rg_w_i, rg_b_i, rg_lam, odd_w_out, ffn_w_up, ffn_w_down):
    nb, seq, _ = x_prompt.shape
    ns = x_sample.shape[0]
    row = lambda x: x.reshape(1, -1)

    even_w_in_b = _reorder_cols(even_w_in, 1536, 1536 + GLA_RANK)
    odd_w_in_b = _reorder_cols(odd_w_in, 1536, 1536 + 2 * H_C)
    even_w_out_b = even_w_out.astype(BF16)
    odd_w_out_b = odd_w_out.astype(BF16)
    w_up_b = ffn_w_up.astype(BF16)
    w_down_b = ffn_w_down.astype(BF16)
    s_gla = state_gla.reshape(N_EVEN, ns, H_A * DK_A, DV_A)
    s_ret = state_ret.reshape(N_EVEN, ns, H_B * DK_B, DV_B)
    s_c = state_mlstm_C.reshape(N_ODD, ns, H_C * DK_C, DV_C)
    cos_p, sin_p = _rope_tables(jnp.arange(seq, dtype=jnp.int32))
    cos_s, sin_s = _rope_tables(PAST_LEN + jnp.arange(1, dtype=jnp.int32))

    xp = x_prompt.reshape(nb * seq, D_MODEL)
    xs = x_sample.reshape(ns, D_MODEL)
    st = {k: [] for k in ("gla_p", "ret_p", "c_p", "n_p", "m_p", "h_p", "conv_p",
                          "gla_s", "ret_s", "c_s", "n_s", "m_s", "h_s", "conv_s")}
    for layer in range(DEPTH):
        g_mix = row(norm_mix[layer])
        final = layer == DEPTH - 1
        if layer % 2 == 0:
            e = layer // 2
            w_in, w_out, idx = even_w_in_b, even_w_out_b, e
            w_lr = jnp.pad(even_w_lr[e], ((0, LANE - GLA_RANK), (0, 0)))
            small = (w_lr, row(even_b_lr[e]), row(even_gn_gla[e]), row(even_gn_ret[e]))
            o_p, sg, sr = _even_prompt(_norm_proj(xp, g_mix, w_in, idx, 512), cos_p, sin_p,
                                       *small, nb, seq)
            st["gla_p"].append(sg.reshape(nb, H_A, DK_A, DV_A))
            st["ret_p"].append(sr.reshape(nb, H_B, DK_B, DV_B))
            o_s, sg, sr = _even_step(_norm_proj(xs, g_mix, w_in, idx, ns), cos_s, sin_s, *small,
                                     s_gla, s_ret, e)
            st["gla_s"].append(sg.reshape(ns, H_A, DK_A, DV_A))
            st["ret_s"].append(sr.reshape(ns, H_B, DK_B, DV_B))
        else:
            o = layer // 2
            w_in, w_out, idx = odd_w_in_b, odd_w_out_b, o
            small = (_pad_lanes(ml_b_i[o]), _pad_lanes(ml_b_f[o], H_C), row(odd_gn_ml[o]),
                     rg_conv_w[o], row(rg_conv_b[o]), _block_diag(rg_w_r[o]), row(rg_b_r[o]),
                     _block_diag(rg_w_i[o]), row(rg_b_i[o]), row(rg_lam[o]))
            o_p, c_, n_, m_, h_, cv = _odd_prompt(_norm_proj(xp, g_mix, w_in, idx, 512), *small,
                                                  nb, seq)
            st["c_p"].append(c_.reshape(nb, H_C, DK_C, DV_C))
            st["n_p"].append(n_)
            st["m_p"].append(m_[:, 0, :H_C])
            st["h_p"].append(h_[:, 0, :])
            st["conv_p"].append(cv)
            o_s, c_, n_, m_, h_, cv = _odd_step(
                _norm_proj(xs, g_mix, w_in, idx, ns), *small,
                s_c,
                state_mlstm_n[o].reshape(ns, H_C * DK_C),
                jnp.pad(state_mlstm_m[o], ((0, 0), (0, LANE - H_C))),
                state_rglru_h[o],
                jnp.swapaxes(state_rglru_conv[o], 0, 1), o)
            st["c_s"].append(c_.reshape(ns, H_C, DK_C, DV_C))
            st["n_s"].append(n_.reshape(ns, H_C, DK_C))
            st["m_s"].append(m_[:, :H_C])
            st["h_s"].append(h_)
            st["conv_s"].append(jnp.swapaxes(cv, 0, 1))
        ffn = (row(norm_ffn[layer]), w_up_b, w_down_b, layer, row(norm_final))
        xp = _out_ffn(xp, o_p, w_out, idx, *ffn, 512, final)
        xs = _out_ffn(xs, o_s, w_out, idx, *ffn, ns, final)

    stack = lambda name: jnp.stack(st[name])
    return (xp.reshape(nb, seq, D_MODEL), xs.reshape(ns, 1, D_MODEL),
            stack("gla_p"), stack("ret_p"), stack("c_p"), stack("n_p"), stack("m_p"),
            stack("h_p"), stack("conv_p"),
            stack("gla_s"), stack("ret_s"), stack("c_s"), stack("n_s"), stack("m_s"),
            stack("h_s"), stack("conv_s"))
```

```python
import functools

import numpy as np
import jax
import jax.numpy as jnp
from jax import lax
from jax.experimental import pallas as pl
from jax.experimental.pallas import tpu as pltpu

F32 = jnp.float32
BF16 = jnp.bfloat16

D_MODEL = 1024
DEPTH = 4
PAST_LEN = 16384
N_EVEN = (DEPTH + 1) // 2
N_ODD = DEPTH // 2
D_HALF = D_MODEL // 2
H_A = 4
DV_A = D_HALF // H_A
DK_A = DV_A // 2
GLA_RANK = 16
GLA_TAU = 16.0
H_B = 4
DV_B = D_HALF // H_B
DK_B = DV_B // 2
ROPE_BASE = 10000.0
H_C = 4
DK_C = D_HALF // H_C
DV_C = D_HALF // H_C
D_RG = D_HALF
RG_BLOCKS = 8
RG_BW = D_RG // RG_BLOCKS
RG_C = 8.0
CONV_W = 4
D_FF = (8 * D_MODEL + 3 * 256 - 1) // (3 * 256) * 256
EPS = 1e-6

LANE = 128
SUBLANE = 8
D_PROJ = 3072 + LANE
GATE_OFF = 3072
EVEN_CHUNK = 64
GLA_SUB = 16
ODD_CHUNK = 128
PROMPT_SEQS = 4
STEP_BB = 8
VMEM_LIMIT = 56 * 1024 * 1024

LOG_GAMMA = [float(np.log1p(-np.exp2(-5.0 - h))) for h in range(H_B)]


def _dot(a, b):
    return jnp.dot(a.astype(BF16), b.astype(BF16), preferred_element_type=F32)


def _dot_nt(a, b):
    return lax.dot_general(a.astype(BF16), b.astype(BF16), (((1,), (1,)), ((), ())),
                           preferred_element_type=F32)


def _dot_f32(a, b):
    return jnp.dot(a, b, preferred_element_type=F32, precision=lax.Precision.HIGHEST)


def _rms(x, g):
    return x * lax.rsqrt(jnp.mean(x * x, axis=-1, keepdims=True) + EPS) * g


def _softplus(x):
    return jnp.maximum(x, 0.0) + jnp.log1p(jnp.exp(-jnp.abs(x)))


def _log_sigmoid(x):
    return -_softplus(-x)


def _sigmoid(x):
    return 0.5 * jnp.tanh(0.5 * x) + 0.5


def _silu(x):
    return x * _sigmoid(x)


def _gelu_tanh(x):
    return 0.5 * x * (1.0 + jnp.tanh(0.7978845608028654 * (x + 0.044715 * x * x * x)))


def _iota(shape, dim):
    return lax.broadcasted_iota(jnp.int32, shape, dim)


def _tri(c):
    return (_iota((c, c), 0) >= _iota((c, c), 1)).astype(F32)


def _stack_heads(x, n_heads, dk):
    head = _iota(x.shape, 1) // dk
    return jnp.concatenate([jnp.where(head == h, x, 0.0) for h in range(n_heads)], axis=0)


def _diag_lanes(y, n_heads, rows, cols):
    return jnp.concatenate(
        [y[h * rows:(h + 1) * rows, h * cols:(h + 1) * cols] for h in range(n_heads)], axis=1)


def _diag_rows(y, n_heads, rows, cols):
    return jnp.concatenate(
        [y[h * rows:(h + 1) * rows, h * cols:(h + 1) * cols] for h in range(n_heads)], axis=0)


def _head_select(idx, vals):
    out = jnp.full(idx.shape, vals[-1], F32)
    for h in range(len(vals) - 2, -1, -1):
        out = jnp.where(idx == h, vals[h], out)
    return out


def _swap_halves(x, dk):
    half = dk // 2
    parts = []
    for c in range(x.shape[1] // LANE):
        xs = x[:, c * LANE:(c + 1) * LANE]
        lo = _iota(xs.shape, 1) % dk < half
        parts.append(jnp.where(lo, pltpu.roll(xs, LANE - half, 1), pltpu.roll(xs, half, 1)))
    return jnp.concatenate(parts, axis=1)


def _rope(x, cos, sin_signed, dk):
    return x * cos + _swap_halves(x, dk) * sin_signed


def _head_rmsnorm(o, n_heads, d):
    parts = []
    for h in range(n_heads):
        oh = o[:, h * d:(h + 1) * d]
        parts.append(oh * lax.rsqrt(jnp.mean(oh * oh, axis=-1, keepdims=True) + EPS))
    return jnp.concatenate(parts, axis=1)


def _head_groupnorm(o, n_heads, d):
    parts = []
    for h in range(n_heads):
        oh = o[:, h * d:(h + 1) * d]
        c = oh - jnp.mean(oh, axis=-1, keepdims=True)
        parts.append(c * lax.rsqrt(jnp.mean(c * c, axis=-1, keepdims=True) + EPS))
    return jnp.concatenate(parts, axis=1)


def _params(*sem):
    return pltpu.CompilerParams(dimension_semantics=sem, vmem_limit_bytes=VMEM_LIMIT)


def _norm_proj_kernel(x_ref, g_ref, w_ref, o_ref):
    o_ref[...] = _dot(_rms(x_ref[...], g_ref[...]), w_ref[...])


def _resident(shape, index):
    return pl.BlockSpec(shape, lambda i: index, pipeline_mode=pl.Buffered(1))


def _norm_proj(x, g, w_stack, idx, tm):
    m = x.shape[0]
    n = w_stack.shape[2]
    return pl.pallas_call(
        _norm_proj_kernel,
        grid=(m // tm,),
        in_specs=[pl.BlockSpec((tm, D_MODEL), lambda i: (i, 0)),
                  _resident((1, D_MODEL), (0, 0)),
                  _resident((None, D_MODEL, n), (idx, 0, 0))],
        out_specs=pl.BlockSpec((tm, n), lambda i: (i, 0)),
        out_shape=jax.ShapeDtypeStruct((m, n), F32),
        compiler_params=_params("parallel"),
        name="norm_proj",
    )(x, g, w_stack)


def _out_ffn_kernel(x_ref, o_ref, wo_ref, g_ref, wg_ref, wu_ref, wd_ref, gf_ref, y_ref, *, final):
    xn = x_ref[...] + _dot(o_ref[...], wo_ref[...])
    hn = _rms(xn, g_ref[...]).astype(BF16)
    gate = _dot(hn, wg_ref[...])
    up = _dot(hn, wu_ref[...])
    y = xn + _dot(_silu(gate) * up, wd_ref[...])
    if final:
        y = _rms(y, gf_ref[...])
    y_ref[...] = y


def _out_ffn(x, o, w_out_stack, idx, g_ffn, w_up_stack, w_down_stack, layer, g_final, tm, final):
    m = x.shape[0]
    return pl.pallas_call(
        functools.partial(_out_ffn_kernel, final=final),
        grid=(m // tm,),
        in_specs=[pl.BlockSpec((tm, D_MODEL), lambda i: (i, 0)),
                  pl.BlockSpec((tm, D_MODEL), lambda i: (i, 0)),
                  _resident((None, D_MODEL, D_MODEL), (idx, 0, 0)),
                  _resident((1, D_MODEL), (0, 0)),
                  _resident((None, D_MODEL, D_FF), (layer, 0, 0)),
                  _resident((None, D_MODEL, D_FF), (layer, 0, 1)),
                  _resident((None, D_FF, D_MODEL), (layer, 0, 0)),
                  _resident((1, D_MODEL), (0, 0))],
        out_specs=pl.BlockSpec((tm, D_MODEL), lambda i: (i, 0)),
        out_shape=jax.ShapeDtypeStruct((m, D_MODEL), F32),
        compiler_params=_params("parallel"),
        name="out_ffn",
    )(x, o, w_out_stack, g_ffn, w_up_stack, w_up_stack, w_down_stack, g_final)


def _gla_logf(glr, wlr_ref, blr_ref):
    return _log_sigmoid(_dot_f32(glr, wlr_ref[...]) + blr_ref[...]) * (1.0 / GLA_TAU)


def _project_chunks(x, g_ref, w_ref, proj_ref):
    ps, c, _ = x.shape
    y = _dot(_rms(x.reshape(ps * c, D_MODEL), g_ref[...]), w_ref[...])
    proj_ref[...] = y.reshape(ps, c, D_PROJ)


def _even_prompt_kernel(xa_ref, xb_ref, g_ref, w_ref, cos_ref, sin_ref, wlr_ref, blr_ref, gng_ref,
                        gnr_ref, o_ref, sg_ref, sr_ref, p0_ref, p1_ref):
    c = EVEN_CHUNK

    @pl.when(pl.program_id(1) == 0)
    def _():
        sg_ref[...] = jnp.zeros_like(sg_ref)
        sr_ref[...] = jnp.zeros_like(sr_ref)
        _project_chunks(xa_ref[:, 0:c, :], g_ref, w_ref, p0_ref)

    def scan(p_ref, lo):
        for s in range(PROMPT_SEQS):
            _even_chunk(p_ref.at[s], cos_ref.at[lo:lo + c], sin_ref.at[lo:lo + c], wlr_ref,
                        blr_ref, gng_ref, gnr_ref, o_ref.at[s, lo:lo + c], sg_ref.at[s],
                        sr_ref.at[s])

    scan(p0_ref, 0)
    _project_chunks(xa_ref[:, c:2 * c, :], g_ref, w_ref, p1_ref)
    scan(p1_ref, c)
    _project_chunks(xb_ref[...], g_ref, w_ref, p0_ref)


def _even_chunk(p_ref, cos_ref, sin_ref, wlr_ref, blr_ref, gng_ref, gnr_ref, o_ref, sg_ref, sr_ref):
    c = EVEN_CHUNK
    r_sub = GLA_SUB

    q = p_ref[:, 0:256] * DK_A ** -0.5
    k = p_ref[:, 256:512]
    v = p_ref[:, 512:1024]
    gg = p_ref[:, 1024:1536]
    logf = _gla_logf(p_ref[:, GATE_OFF:GATE_OFF + LANE], wlr_ref, blr_ref)
    b = _dot_f32(_tri(c), logf)
    s_gla = sg_ref[...]
    inter = _dot(_stack_heads(q * jnp.exp(b), H_A, DK_A), s_gla)
    o_a = jnp.concatenate([inter[h * c:(h + 1) * c] for h in range(H_A)], axis=1)

    ind = (_iota((H_A * DK_A, H_A * DV_A), 0) // DK_A ==
           _iota((H_A * DK_A, H_A * DV_A), 1) // DV_A).astype(BF16)
    row = _iota((r_sub, H_A * DK_A), 0)
    blocks = []
    for i in range(c // r_sub):
        lo = i * r_sub
        qb = q[lo:lo + r_sub]
        bb = b[lo:lo + r_sub]
        ws = []
        for s in range(r_sub):
            e = jnp.exp(jnp.minimum(bb - b[lo + s:lo + s + 1], 0.0))
            ws.append(jnp.where(row >= s, qb * k[lo + s:lo + s + 1] * e, 0.0))
        rep = _dot(jnp.concatenate(ws, axis=0), ind)
        ob = rep[0:r_sub] * v[lo:lo + 1]
        for s in range(1, r_sub):
            ob = ob + rep[s * r_sub:(s + 1) * r_sub] * v[lo + s:lo + s + 1]
        if i > 0:
            ref_b = b[lo - 1:lo]
            qi = qb * jnp.exp(bb - ref_b)
            ki = k[0:lo] * jnp.exp(ref_b - b[0:lo])
            sc = _dot_nt(_stack_heads(qi, H_A, DK_A), ki)
            ob = ob + _diag_lanes(_dot(sc, v[0:lo]), H_A, r_sub, DV_A)
        blocks.append(ob)
    o_a = o_a + jnp.concatenate(blocks, axis=0)
    o_a = _head_rmsnorm(o_a, H_A, DV_A) * gng_ref[...] * _silu(gg)

    b_t = b.T
    b_last = b_t[:, c - 1:c]
    kd_t = k.T * jnp.exp(b_last - b_t)
    upd = _diag_rows(_dot(kd_t, v), H_A, DK_A, DV_A)
    sg_ref[...] = s_gla * jnp.exp(b_last) + upd

    cos = cos_ref[...]
    sin = sin_ref[...]
    rq = _rope(p_ref[:, 1536:1792], cos, sin, DK_B) * DK_B ** -0.5
    rk = _rope(p_ref[:, 1792:2048], cos, sin, DK_B)
    rv = p_ref[:, 2048:2560]
    rg = p_ref[:, 2560:3072]
    s_ret = sr_ref[...]
    t_row = _iota((H_B * c, c), 0)
    lg_row = _head_select(t_row // c, LOG_GAMMA)
    t_loc = (t_row % c).astype(F32)
    s_loc = _iota((H_B * c, c), 1).astype(F32)
    decay = jnp.where(t_loc >= s_loc, jnp.exp(jnp.maximum(t_loc - s_loc, 0.0) * lg_row), 0.0)
    q_dec = jnp.exp((t_loc[:, 0:1] + 1.0) * lg_row[:, 0:1])
    qs = _stack_heads(rq, H_B, DK_B)
    scores = _dot_nt(qs, rk) * decay
    o_b = _diag_lanes(_dot(scores, rv), H_B, c, DV_B)
    inter_b = _dot(qs * q_dec, s_ret)
    o_b = o_b + jnp.concatenate([inter_b[h * c:(h + 1) * c] for h in range(H_B)], axis=1)
    o_b = _head_groupnorm(o_b, H_B, DV_B) * gnr_ref[...] * _silu(rg)

    lg_lane = _head_select(_iota((c, H_B * DK_B), 1) // DK_B, LOG_GAMMA)
    k_dec = jnp.exp((c - 1.0 - _iota((c, H_B * DK_B), 0).astype(F32)) * lg_lane)
    upd_b = _diag_rows(_dot((rk * k_dec).T, rv), H_B, DK_B, DV_B)
    g_chunk = jnp.exp(c * _head_select(_iota((H_B * DK_B, DV_B), 0) // DK_B, LOG_GAMMA))
    sr_ref[...] = s_ret * g_chunk + upd_b

    o_ref[...] = jnp.concatenate([o_a, o_b], axis=1).astype(BF16)


def _even_prompt(x, g_mix, w_stack, idx, cos, sin, w_lr, b_lr, gn_gla, gn_ret, nb, seq):
    c = EVEN_CHUNK
    ps = PROMPT_SEQS
    steps = seq // (2 * c)
    last = seq // c - 1
    small = lambda shape: pl.BlockSpec(shape, lambda b, n: (0, 0))
    per_seq = lambda rows, cols: pl.BlockSpec((ps, rows, cols), lambda b, n: (b, 0, 0))
    x = x.reshape(nb, seq, D_MODEL)
    o, sg, sr = pl.pallas_call(
        _even_prompt_kernel,
        grid=(nb // ps, steps),
        in_specs=[pl.BlockSpec((ps, 2 * c, D_MODEL), lambda b, n: (b, n, 0)),
                  pl.BlockSpec((ps, c, D_MODEL),
                               lambda b, n: (b, jnp.minimum(2 * n + 2, last), 0)),
                  small((1, D_MODEL)),
                  pl.BlockSpec((None, D_MODEL, D_PROJ), lambda b, n: (idx, 0, 0),
                               pipeline_mode=pl.Buffered(1)),
                  pl.BlockSpec((2 * c, H_B * DK_B), lambda b, n: (n, 0)),
                  pl.BlockSpec((2 * c, H_B * DK_B), lambda b, n: (n, 0)),
                  small((LANE, H_A * DK_A)), small((1, H_A * DK_A)),
                  small((1, D_HALF)), small((1, D_HALF))],
        out_specs=[pl.BlockSpec((ps, 2 * c, D_MODEL), lambda b, n: (b, n, 0)),
                   per_seq(H_A * DK_A, DV_A), per_seq(H_B * DK_B, DV_B)],
        out_shape=[jax.ShapeDtypeStruct((nb, seq, D_MODEL), BF16),
                   jax.ShapeDtypeStruct((nb, H_A * DK_A, DV_A), F32),
                   jax.ShapeDtypeStruct((nb, H_B * DK_B, DV_B), F32)],
        scratch_shapes=[pltpu.VMEM((ps, c, D_PROJ), F32), pltpu.VMEM((ps, c, D_PROJ), F32)],
        compiler_params=_params("parallel", "arbitrary"),
        name="even_prompt",
    )(x, x, g_mix, w_stack, cos, sin, w_lr, b_lr, gn_gla, gn_ret)
    return o.reshape(nb * seq, D_MODEL), sg, sr


def _rg_gates(xc, wr_ref, br_ref, wi_ref, bi_ref, lam_ref):
    r = _sigmoid(_dot(xc, wr_ref[...]) + br_ref[...])
    i = _sigmoid(_dot(xc, wi_ref[...]) + bi_ref[...])
    log_a = -RG_C * r * _softplus(-lam_ref[...])
    a = jnp.exp(log_a)
    u = jnp.sqrt(jnp.tanh(-log_a) * (a * a + 1.0)) * (i * xc)
    return a, u


def _odd_prompt_kernel(xa_ref, xb_ref, g_ref, w_ref, big_ref, bfg_ref, gnm_ref, cw_ref, cb_ref,
                       wr_ref, br_ref, wi_ref, bi_ref, lam_ref, o_ref, cst_ref, nst_ref, mst_ref,
                       hst_ref, conv_ref, xpad_ref, p0_ref, p1_ref):
    c = ODD_CHUNK

    @pl.when(pl.program_id(1) == 0)
    def _():
        cst_ref[...] = jnp.zeros_like(cst_ref)
        nst_ref[...] = jnp.zeros_like(nst_ref)
        mst_ref[...] = jnp.zeros_like(mst_ref)
        hst_ref[...] = jnp.zeros_like(hst_ref)
        xpad_ref[:, 0:8, :] = jnp.zeros((PROMPT_SEQS, 8, D_RG), F32)
        _project_chunks(xa_ref[:, 0:c, :], g_ref, w_ref, p0_ref)

    def scan(p_ref, lo):
        for s in range(PROMPT_SEQS):
            _odd_chunk(p_ref.at[s], big_ref, bfg_ref, gnm_ref, cw_ref, cb_ref, wr_ref, br_ref,
                       wi_ref, bi_ref, lam_ref, o_ref.at[s, lo:lo + c], cst_ref.at[s],
                       nst_ref.at[s], mst_ref.at[s], hst_ref.at[s], conv_ref.at[s], xpad_ref.at[s])

    scan(p0_ref, 0)
    _project_chunks(xa_ref[:, c:2 * c, :], g_ref, w_ref, p1_ref)
    scan(p1_ref, c)
    _project_chunks(xb_ref[...], g_ref, w_ref, p0_ref)


def _odd_chunk(p_ref, big_ref, bfg_ref, gnm_ref, cw_ref, cb_ref, wr_ref, br_ref, wi_ref,
               bi_ref, lam_ref, o_ref, cst_ref, nst_ref, mst_ref, hst_ref, conv_ref, xpad_ref):
    c = ODD_CHUNK

    gates = p_ref[:, GATE_OFF:GATE_OFF + LANE]
    ig = gates + big_ref[...]
    lf = _log_sigmoid(gates + bfg_ref[...])
    b_all = pltpu.roll(_dot_f32(_tri(c), lf), LANE - H_C, 1)
    a_all = ig - b_all
    a_rows = a_all.T
    causal = _iota((c, c), 0) >= _iota((c, c), 1)
    m_prev = mst_ref[...]
    m_new = m_prev
    lane = _iota((1, LANE), 1)
    outs = []
    for h in range(H_C):
        sl = slice(h * DK_C, (h + 1) * DK_C)
        qh = p_ref[:, sl] * DK_C ** -0.5
        kh = p_ref[:, 512 + h * DK_C:512 + (h + 1) * DK_C]
        vh = p_ref[:, 1024 + h * DV_C:1024 + (h + 1) * DV_C]
        bcol = b_all[:, h:h + 1]
        acol = a_all[:, h:h + 1]
        arow = a_rows[h:h + 1, :]
        mp = m_prev[:, h:h + 1]
        cmax = jnp.max(jnp.where(causal, arow, -jnp.inf), axis=1, keepdims=True)
        m_t = jnp.maximum(bcol + mp, bcol + cmax)
        w_state = jnp.exp(bcol + mp - m_t)
        w_intra = jnp.where(causal, jnp.exp(jnp.minimum(bcol + arow - m_t, 0.0)), 0.0)
        c_h = cst_ref[sl, :]
        n_h = nst_ref[h:h + 1, :]
        scores = _dot_nt(qh, kh) * w_intra
        num = w_state * _dot(qh, c_h) + _dot(scores, vh)
        den = w_state * jnp.sum(qh * n_h, axis=1, keepdims=True) + \
            jnp.sum(scores, axis=1, keepdims=True)
        outs.append(num / jnp.maximum(jnp.abs(den), jnp.exp(-m_t)))
        m_last = m_t[c - 1:c]
        b_last = bcol[c - 1:c]
        w_s_last = jnp.exp(b_last + mp - m_last)
        kw = kh * jnp.exp(b_last + acol - m_last)
        cst_ref[sl, :] = c_h * w_s_last + _dot(kw.T, vh)
        nst_ref[h:h + 1, :] = n_h * w_s_last + jnp.sum(kw, axis=0, keepdims=True)
        m_new = jnp.where(lane == h, m_last, m_new)
    mst_ref[...] = m_new
    o_c = _head_rmsnorm(jnp.concatenate(outs, axis=1), H_C, DV_C) * gnm_ref[...] * \
        _sigmoid(p_ref[:, 1536:2048])

    rx = p_ref[:, 2048:2560]
    xpad_ref[8:8 + c, :] = rx
    xc = cb_ref[...] + cw_ref[3:4, :] * rx
    for j in range(1, CONV_W):
        xc = xc + cw_ref[3 - j:4 - j, :] * xpad_ref[8 - j:8 - j + c, :]
    conv_ref[...] = xpad_ref[c + 8 - (CONV_W - 1):c + 8, :]
    xpad_ref[0:8, :] = xpad_ref[c:c + 8, :]
    a, u = _rg_gates(xc, wr_ref, br_ref, wi_ref, bi_ref, lam_ref)
    a = a.reshape(c // SUBLANE, SUBLANE, D_RG)
    u = u.reshape(c // SUBLANE, SUBLANE, D_RG)
    t_idx = _iota(a.shape, 1)
    s = 1
    while s < SUBLANE:
        keep = t_idx >= s
        u = jnp.where(keep, a * pltpu.roll(u, s, 1) + u, u)
        a = jnp.where(keep, a * pltpu.roll(a, s, 1), a)
        s *= 2
    carry = hst_ref[...]
    groups = []
    for g in range(c // SUBLANE):
        hg = a[g] * carry + u[g]
        carry = hg[SUBLANE - 1:SUBLANE]
        groups.append(hg)
    hs = jnp.concatenate(groups, axis=0)
    hst_ref[...] = carry
    o_d = hs * _gelu_tanh(p_ref[:, 2560:3072])

    o_ref[...] = jnp.concatenate([o_c, o_d], axis=1).astype(BF16)


def _odd_prompt(x, g_mix, w_stack, idx, b_ig, b_fg, gn_ml, conv_w, conv_b, w_r, b_r, w_i, b_i, lam,
                nb, seq):
    c = ODD_CHUNK
    ps = PROMPT_SEQS
    steps = seq // (2 * c)
    last = seq // c - 1
    small = lambda shape: pl.BlockSpec(shape, lambda b, n: (0, 0))
    per_seq = lambda rows, cols: pl.BlockSpec((ps, rows, cols), lambda b, n: (b, 0, 0))
    x = x.reshape(nb, seq, D_MODEL)
    outs = pl.pallas_call(
        _odd_prompt_kernel,
        grid=(nb // ps, steps),
        in_specs=[pl.BlockSpec((ps, 2 * c, D_MODEL), lambda b, n: (b, n, 0)),
                  pl.BlockSpec((ps, c, D_MODEL),
                               lambda b, n: (b, jnp.minimum(2 * n + 2, last), 0)),
                  small((1, D_MODEL)),
                  pl.BlockSpec((None, D_MODEL, D_PROJ), lambda b, n: (idx, 0, 0),
                               pipeline_mode=pl.Buffered(1)),
                  small((1, LANE)), small((1, LANE)), small((1, D_HALF)),
                  small((CONV_W, D_RG)), small((1, D_RG)),
                  small((D_RG, D_RG)), small((1, D_RG)), small((D_RG, D_RG)), small((1, D_RG)),
                  small((1, D_RG))],
        out_specs=[pl.BlockSpec((ps, 2 * c, D_MODEL), lambda b, n: (b, n, 0)),
                   per_seq(H_C * DK_C, DV_C), per_seq(H_C, DK_C), per_seq(1, LANE),
                   per_seq(1, D_RG), per_seq(CONV_W - 1, D_RG)],
        out_shape=[jax.ShapeDtypeStruct((nb, seq, D_MODEL), BF16),
                   jax.ShapeDtypeStruct((nb, H_C * DK_C, DV_C), F32),
                   jax.ShapeDtypeStruct((nb, H_C, DK_C), F32),
                   jax.ShapeDtypeStruct((nb, 1, LANE), F32),
                   jax.ShapeDtypeStruct((nb, 1, D_RG), F32),
                   jax.ShapeDtypeStruct((nb, CONV_W - 1, D_RG), F32)],
        scratch_shapes=[pltpu.VMEM((ps, c + 8, D_RG), F32),
                        pltpu.VMEM((ps, c, D_PROJ), F32), pltpu.VMEM((ps, c, D_PROJ), F32)],
        compiler_params=_params("parallel", "arbitrary"),
        name="odd_prompt",
    )(x, x, g_mix, w_stack, b_ig, b_fg, gn_ml, conv_w, conv_b, w_r, b_r, w_i, b_i, lam)
    return (outs[0].reshape(nb * seq, D_MODEL),) + tuple(outs[1:])


def _rank1_step(s_in_ref, s_out_ref, o_scr_ref, q_t, k_t, d_t, v, n_heads, dk, dv):
    bb = v.shape[0]
    for j in range(bb):
        vst = jnp.concatenate(
            [jnp.broadcast_to(v[j:j + 1, h * dv:(h + 1) * dv], (dk, dv)) for h in range(n_heads)],
            axis=0)
        s_new = s_in_ref[j] * d_t[:, j:j + 1] + k_t[:, j:j + 1] * vst
        s_out_ref[j] = s_new
        prod = q_t[:, j:j + 1] * s_new
        o_scr_ref[j:j + 1, :] = jnp.concatenate(
            [jnp.sum(prod[h * dk:(h + 1) * dk], axis=0, keepdims=True) for h in range(n_heads)],
            axis=1)


def _even_step_kernel(p_ref, cos_ref, sin_ref, wlr_ref, blr_ref, gng_ref, gnr_ref, sg_in_ref,
                      sr_in_ref, o_ref, sg_out_ref, sr_out_ref, oa_ref, ob_ref):
    bb = STEP_BB
    q = p_ref[:, 0:256] * DK_A ** -0.5
    k = p_ref[:, 256:512]
    v = p_ref[:, 512:1024]
    gg = p_ref[:, 1024:1536]
    d = jnp.exp(_gla_logf(p_ref[:, GATE_OFF:GATE_OFF + LANE], wlr_ref, blr_ref))
    _rank1_step(sg_in_ref, sg_out_ref, oa_ref, q.T, k.T, d.T, v, H_A, DK_A, DV_A)
    o_a = _head_rmsnorm(oa_ref[...], H_A, DV_A) * gng_ref[...] * _silu(gg)

    cos = cos_ref[...]
    sin = sin_ref[...]
    rq = _rope(p_ref[:, 1536:1792], cos, sin, DK_B) * DK_B ** -0.5
    rk = _rope(p_ref[:, 1792:2048], cos, sin, DK_B)
    rv = p_ref[:, 2048:2560]
    rg = p_ref[:, 2560:3072]
    gamma_t = jnp.exp(_head_select(_iota((H_B * DK_B, bb), 0) // DK_B, LOG_GAMMA))
    _rank1_step(sr_in_ref, sr_out_ref, ob_ref, rq.T, rk.T, gamma_t, rv, H_B, DK_B, DV_B)
    o_b = _head_groupnorm(ob_ref[...], H_B, DV_B) * gnr_ref[...] * _silu(rg)
    o_ref[...] = jnp.concatenate([o_a, o_b], axis=1).astype(BF16)


def _even_step(proj, cos, sin, w_lr, b_lr, gn_gla, gn_ret, s_gla, s_ret, idx):
    nb = proj.shape[0]
    bb = STEP_BB
    small = lambda shape: pl.BlockSpec(shape, lambda i: (0, 0))
    state = lambda rows, cols: pl.BlockSpec((bb, rows, cols), lambda i: (i, 0, 0))
    state_in = lambda rows, cols: pl.BlockSpec((None, bb, rows, cols), lambda i: (idx, i, 0, 0))
    return pl.pallas_call(
        _even_step_kernel,
        grid=(nb // bb,),
        in_specs=[pl.BlockSpec((bb, D_PROJ), lambda i: (i, 0)),
                  small((1, H_B * DK_B)), small((1, H_B * DK_B)),
                  small((LANE, H_A * DK_A)), small((1, H_A * DK_A)),
                  small((1, D_HALF)), small((1, D_HALF)),
                  state_in(H_A * DK_A, DV_A), state_in(H_B * DK_B, DV_B)],
        out_specs=[pl.BlockSpec((bb, D_MODEL), lambda i: (i, 0)),
                   state(H_A * DK_A, DV_A), state(H_B * DK_B, DV_B)],
        out_shape=[jax.ShapeDtypeStruct((nb, D_MODEL), BF16),
                   jax.ShapeDtypeStruct(s_gla.shape[1:], F32),
                   jax.ShapeDtypeStruct(s_ret.shape[1:], F32)],
        scratch_shapes=[pltpu.VMEM((bb, D_HALF), F32), pltpu.VMEM((bb, D_HALF), F32)],
        compiler_params=_params("parallel"),
        name="even_step",
    )(proj, cos, sin, w_lr, b_lr, gn_gla, gn_ret, s_gla, s_ret)


def _odd_step_kernel(p_ref, big_ref, bfg_ref, gnm_ref, cw_ref, cb_ref, wr_ref, br_ref, wi_ref,
                     bi_ref, lam_ref, c_in_ref, n_in_ref, m_in_ref, h_in_ref, conv_in_ref,
                     o_ref, c_out_ref, n_out_ref, m_out_ref, h_out_ref, conv_out_ref, num_ref):
    bb = STEP_BB
    gates = p_ref[:, GATE_OFF:GATE_OFF + LANE]
    ig = gates + big_ref[...]
    lf = pltpu.roll(_log_sigmoid(gates + bfg_ref[...]), LANE - H_C, 1)
    m_prev = m_in_ref[...]
    m_new = jnp.maximum(lf + m_prev, ig)
    valid = _iota((bb, LANE), 1) < H_C
    w_s = jnp.where(valid, jnp.exp(lf + m_prev - m_new), 0.0)
    w_i = jnp.where(valid, jnp.exp(ig - m_new), 0.0)
    m_out_ref[...] = jnp.where(valid, m_new, 0.0)
    expand = (_iota((LANE, H_C * DK_C), 0) == _iota((LANE, H_C * DK_C), 1) // DK_C).astype(F32)
    w_s_wide = _dot_f32(w_s, expand)
    w_i_wide = _dot_f32(w_i, expand)
    q = p_ref[:, 0:512] * DK_C ** -0.5
    kw = p_ref[:, 512:1024] * w_i_wide
    v = p_ref[:, 1024:1536]
    _rank1_step(c_in_ref, c_out_ref, num_ref, q.T, kw.T, w_s_wide.T, v, H_C, DK_C, DV_C)
    n_new = n_in_ref[...] * w_s_wide + kw
    n_out_ref[...] = n_new
    num = num_ref[...]
    qn = q * n_new
    outs = []
    for h in range(H_C):
        sl = slice(h * DK_C, (h + 1) * DK_C)
        den = jnp.sum(qn[:, sl], axis=1, keepdims=True)
        outs.append(num[:, sl] / jnp.maximum(jnp.abs(den), jnp.exp(-m_new[:, h:h + 1])))
    o_c = _head_rmsnorm(jnp.concatenate(outs, axis=1), H_C, DV_C) * gnm_ref[...] * \
        _sigmoid(p_ref[:, 1536:2048])

    rx = p_ref[:, 2048:2560]
    xc = cb_ref[...] + cw_ref[3:4, :] * rx
    for j in range(CONV_W - 1):
        xc = xc + cw_ref[j:j + 1, :] * conv_in_ref[j]
    for j in range(CONV_W - 2):
        conv_out_ref[j] = conv_in_ref[j + 1]
    conv_out_ref[CONV_W - 2] = rx
    a, u = _rg_gates(xc, wr_ref, br_ref, wi_ref, bi_ref, lam_ref)
    hs = a * h_in_ref[...] + u
    h_out_ref[...] = hs
    o_d = hs * _gelu_tanh(p_ref[:, 2560:3072])
    o_ref[...] = jnp.concatenate([o_c, o_d], axis=1).astype(BF16)


def _odd_step(proj, b_ig, b_fg, gn_ml, conv_w, conv_b, w_r, b_r, w_i, b_i, lam,
              s_c, s_n, s_m, s_h, s_conv, idx):
    nb = proj.shape[0]
    bb = STEP_BB
    small = lambda shape: pl.BlockSpec(shape, lambda i: (0, 0))
    rows = lambda cols: pl.BlockSpec((bb, cols), lambda i: (i, 0))
    c_spec = pl.BlockSpec((bb, H_C * DK_C, DV_C), lambda i: (i, 0, 0))
    c_in_spec = pl.BlockSpec((None, bb, H_C * DK_C, DV_C), lambda i: (idx, i, 0, 0))
    conv_spec = pl.BlockSpec((CONV_W - 1, bb, D_RG), lambda i: (0, i, 0))
    return pl.pallas_call(
        _odd_step_kernel,
        grid=(nb // bb,),
        in_specs=[rows(D_PROJ),
                  small((1, LANE)), small((1, LANE)), small((1, D_HALF)),
                  small((CONV_W, D_RG)), small((1, D_RG)),
                  small((D_RG, D_RG)), small((1, D_RG)), small((D_RG, D_RG)), small((1, D_RG)),
                  small((1, D_RG)),
                  c_in_spec, rows(H_C * DK_C), rows(LANE), rows(D_RG), conv_spec],
        out_specs=[rows(D_MODEL), c_spec, rows(H_C * DK_C), rows(LANE), rows(D_RG), conv_spec],
        out_shape=[jax.ShapeDtypeStruct((nb, D_MODEL), BF16),
                   jax.ShapeDtypeStruct(s_c.shape[1:], F32),
                   jax.ShapeDtypeStruct(s_n.shape, F32),
                   jax.ShapeDtypeStruct(s_m.shape, F32),
                   jax.ShapeDtypeStruct(s_h.shape, F32),
                   jax.ShapeDtypeStruct(s_conv.shape, F32)],
        scratch_shapes=[pltpu.VMEM((bb, D_HALF), F32)],
        compiler_params=_params("parallel"),
        name="odd_step",
    )(proj, b_ig, b_fg, gn_ml, conv_w, conv_b, w_r, b_r, w_i, b_i, lam, s_c, s_n, s_m, s_h, s_conv)


def _rope_tables(pos):
    half = DK_B // 2
    inv = ROPE_BASE ** (-jnp.arange(half, dtype=F32) / half)
    ang = pos.astype(F32)[:, None] * inv[None, :]
    cos = jnp.tile(jnp.cos(ang), (1, 2 * H_B))
    sin = jnp.tile(jnp.concatenate([-jnp.sin(ang), jnp.sin(ang)], axis=1), (1, H_B))
    return cos, sin


def _reorder_cols(w, gate_lo, gate_hi):
    pad = jnp.zeros(w.shape[:-1] + (LANE - (gate_hi - gate_lo),), w.dtype)
    return jnp.concatenate([w[..., :gate_lo], w[..., gate_hi:], w[..., gate_lo:gate_hi], pad],
                           axis=-1).astype(BF16)


def _pad_lanes(x, offset=0):
    return jnp.pad(x.reshape(1, -1), ((0, 0), (offset, LANE - offset - x.size)))


def _block_diag(w):
    return jax.scipy.linalg.block_diag(*[w[i] for i in range(RG_BLOCKS)]).astype(BF16)


def kernel(x_prompt, x_sample, state_gla, state_ret, state_mlstm_C, state_mlstm_n, state_mlstm_m, state_rglru_h, state_rglru_conv, norm_mix, norm_ffn, norm_final, even_w_in, even_w_lr, even_b_lr, even_gn_gla, even_gn_ret, even_w_out, odd_w_in, ml_b_i, ml_b_f, odd_gn_ml, rg_conv_w, rg_conv_b, rg_w_r, rg_b_r, rg_w_i, rg_b_i, rg_lam, odd_w_out, ffn_w_up, ffn_w_down):
    nb, seq, _ = x_prompt.shape
    ns = x_sample.shape[0]
    row = lambda x: x.reshape(1, -1)

    even_w_in_b = _reorder_cols(even_w_in, 1536, 1536 + GLA_RANK)
    odd_w_in_b = _reorder_cols(odd_w_in, 1536, 1536 + 2 * H_C)
    even_w_out_b = even_w_out.astype(BF16)
    odd_w_out_b = odd_w_out.astype(BF16)
    w_up_b = ffn_w_up.astype(BF16)
    w_down_b = ffn_w_down.astype(BF16)
    s_gla = state_gla.reshape(N_EVEN, ns, H_A * DK_A, DV_A)
    s_ret = state_ret.reshape(N_EVEN, ns, H_B * DK_B, DV_B)
    s_c = state_mlstm_C.reshape(N_ODD, ns, H_C * DK_C, DV_C)
    cos_p, sin_p = _rope_tables(jnp.arange(seq, dtype=jnp.int32))
    cos_s, sin_s = _rope_tables(PAST_LEN + jnp.arange(1, dtype=jnp.int32))

    xp = x_prompt.reshape(nb * seq, D_MODEL)
    xs = x_sample.reshape(ns, D_MODEL)
    st = {k: [] for k in ("gla_p", "ret_p", "c_p", "n_p", "m_p", "h_p", "conv_p",
                          "gla_s", "ret_s", "c_s", "n_s", "m_s", "h_s", "conv_s")}
    for layer in range(DEPTH):
        g_mix = row(norm_mix[layer])
        final = layer == DEPTH - 1
        if layer % 2 == 0:
            e = layer // 2
            w_in, w_out, idx = even_w_in_b, even_w_out_b, e
            w_lr = jnp.pad(even_w_lr[e], ((0, LANE - GLA_RANK), (0, 0)))
            small = (w_lr, row(even_b_lr[e]), row(even_gn_gla[e]), row(even_gn_ret[e]))
            o_p, sg, sr = _even_prompt(xp, g_mix, w_in, idx, cos_p, sin_p, *small, nb, seq)
            st["gla_p"].append(sg.reshape(nb, H_A, DK_A, DV_A))
            st["ret_p"].append(sr.reshape(nb, H_B, DK_B, DV_B))
            o_s, sg, sr = _even_step(_norm_proj(xs, g_mix, w_in, idx, ns), cos_s, sin_s, *small,
                                     s_gla, s_ret, e)
            st["gla_s"].append(sg.reshape(ns, H_A, DK_A, DV_A))
            st["ret_s"].append(sr.reshape(ns, H_B, DK_B, DV_B))
        else:
            o = layer // 2
            w_in, w_out, idx = odd_w_in_b, odd_w_out_b, o
            small = (_pad_lanes(ml_b_i[o]), _pad_lanes(ml_b_f[o], H_C), row(odd_gn_ml[o]),
                     rg_conv_w[o], row(rg_conv_b[o]), _block_diag(rg_w_r[o]), row(rg_b_r[o]),
                     _block_diag(rg_w_i[o]), row(rg_b_i[o]), row(rg_lam[o]))
            o_p, c_, n_, m_, h_, cv = _odd_prompt(xp, g_mix, w_in, idx, *small, nb, seq)
            st["c_p"].append(c_.reshape(nb, H_C, DK_C, DV_C))
            st["n_p"].append(n_)
            st["m_p"].append(m_[:, 0, :H_C])
            st["h_p"].append(h_[:, 0, :])
            st["conv_p"].append(cv)
            o_s, c_, n_, m_, h_, cv = _odd_step(
                _norm_proj(xs, g_mix, w_in, idx, ns), *small,
                s_c,
                state_mlstm_n[o].reshape(ns, H_C * DK_C),
                jnp.pad(state_mlstm_m[o], ((0, 0), (0, LANE - H_C))),
                state_rglru_h[o],
                jnp.swapaxes(state_rglru_conv[o], 0, 1), o)
            st["c_s"].append(c_.reshape(ns, H_C, DK_C, DV_C))
            st["n_s"].append(n_.reshape(ns, H_C, DK_C))
            st["m_s"].append(m_[:, :H_C])
            st["h_s"].append(h_)
            st["conv_s"].append(jnp.swapaxes(cv, 0, 1))
        ffn = (row(norm_ffn[layer]), w_up_b, w_down_b, layer, row(norm_final))
        xp = _out_ffn(xp, o_p, w_out, idx, *ffn, 512, final)
        xs = _out_ffn(xs, o_s, w_out, idx, *ffn, ns, final)

    stack = lambda name: jnp.stack(st[name])
    return (xp.reshape(nb, seq, D_MODEL), xs.reshape(ns, 1, D_MODEL),
            stack("gla_p"), stack("ret_p"), stack("c_p"), stack("n_p"), stack("m_p"),
            stack("h_p"), stack("conv_p"),
            stack("gla_s"), stack("ret_s"), stack("c_s"), stack("n_s"), stack("m_s"),
            stack("h_s"), stack("conv_s"))
```

```python
import functools

import numpy as np
import jax
import jax.numpy as jnp
from jax import lax
from jax.experimental import pallas as pl
from jax.experimental.pallas import tpu as pltpu

F32 = jnp.float32
BF16 = jnp.bfloat16

D_MODEL = 1024
DEPTH = 4
PAST_LEN = 16384
N_EVEN = (DEPTH + 1) // 2
N_ODD = DEPTH // 2
D_HALF = D_MODEL // 2
H_A = 4
DV_A = D_HALF // H_A
DK_A = DV_A // 2
GLA_RANK = 16
GLA_TAU = 16.0
H_B = 4
DV_B = D_HALF // H_B
DK_B = DV_B // 2
ROPE_BASE = 10000.0
H_C = 4
DK_C = D_HALF // H_C
DV_C = D_HALF // H_C
D_RG = D_HALF
RG_BLOCKS = 8
RG_BW = D_RG // RG_BLOCKS
RG_C = 8.0
CONV_W = 4
D_FF = (8 * D_MODEL + 3 * 256 - 1) // (3 * 256) * 256
EPS = 1e-6

LANE = 128
SUBLANE = 8
D_PROJ = 3072 + LANE
GATE_OFF = 3072
EVEN_CHUNK = 64
GLA_SUB = 16
ODD_CHUNK = 128
EVEN_SEQS = 4
PROMPT_SEQS = 4
STEP_BB = 8
VMEM_LIMIT = 56 * 1024 * 1024

LOG_GAMMA = [float(np.log1p(-np.exp2(-5.0 - h))) for h in range(H_B)]


def _dot(a, b):
    return jnp.dot(a.astype(BF16), b.astype(BF16), preferred_element_type=F32)


def _dot_nt(a, b):
    return lax.dot_general(a.astype(BF16), b.astype(BF16), (((1,), (1,)), ((), ())),
                           preferred_element_type=F32)


def _dot_f32(a, b):
    return jnp.dot(a, b, preferred_element_type=F32, precision=lax.Precision.HIGHEST)


def _rms(x, g):
    return x * lax.rsqrt(jnp.mean(x * x, axis=-1, keepdims=True) + EPS) * g


def _softplus(x):
    return jnp.maximum(x, 0.0) + jnp.log1p(jnp.exp(-jnp.abs(x)))


def _log_sigmoid(x):
    return -_softplus(-x)


def _sigmoid(x):
    return 0.5 * jnp.tanh(0.5 * x) + 0.5


def _silu(x):
    return x * _sigmoid(x)


def _gelu_tanh(x):
    return 0.5 * x * (1.0 + jnp.tanh(0.7978845608028654 * (x + 0.044715 * x * x * x)))


def _iota(shape, dim):
    return lax.broadcasted_iota(jnp.int32, shape, dim)


def _tri(c):
    return (_iota((c, c), 0) >= _iota((c, c), 1)).astype(F32)


def _stack_heads(x, n_heads, dk):
    head = _iota(x.shape, 1) // dk
    return jnp.concatenate([jnp.where(head == h, x, 0.0) for h in range(n_heads)], axis=0)


def _diag_lanes(y, n_heads, rows, cols):
    return jnp.concatenate(
        [y[h * rows:(h + 1) * rows, h * cols:(h + 1) * cols] for h in range(n_heads)], axis=1)


def _diag_rows(y, n_heads, rows, cols):
    return jnp.concatenate(
        [y[h * rows:(h + 1) * rows, h * cols:(h + 1) * cols] for h in range(n_heads)], axis=0)


def _head_select(idx, vals):
    out = jnp.full(idx.shape, vals[-1], F32)
    for h in range(len(vals) - 2, -1, -1):
        out = jnp.where(idx == h, vals[h], out)
    return out


def _swap_halves(x, dk):
    half = dk // 2
    parts = []
    for c in range(x.shape[1] // LANE):
        xs = x[:, c * LANE:(c + 1) * LANE]
        lo = _iota(xs.shape, 1) % dk < half
        parts.append(jnp.where(lo, pltpu.roll(xs, LANE - half, 1), pltpu.roll(xs, half, 1)))
    return jnp.concatenate(parts, axis=1)


def _rope(x, cos, sin_signed, dk):
    return x * cos + _swap_halves(x, dk) * sin_signed


def _head_rmsnorm(o, n_heads, d):
    parts = []
    for h in range(n_heads):
        oh = o[:, h * d:(h + 1) * d]
        parts.append(oh * lax.rsqrt(jnp.mean(oh * oh, axis=-1, keepdims=True) + EPS))
    return jnp.concatenate(parts, axis=1)


def _head_groupnorm(o, n_heads, d):
    parts = []
    for h in range(n_heads):
        oh = o[:, h * d:(h + 1) * d]
        c = oh - jnp.mean(oh, axis=-1, keepdims=True)
        parts.append(c * lax.rsqrt(jnp.mean(c * c, axis=-1, keepdims=True) + EPS))
    return jnp.concatenate(parts, axis=1)


def _params(*sem):
    return pltpu.CompilerParams(dimension_semantics=sem, vmem_limit_bytes=VMEM_LIMIT)


def _norm_proj_kernel(x_ref, g_ref, w_ref, o_ref):
    o_ref[...] = _dot(_rms(x_ref[...], g_ref[...]), w_ref[...])


def _resident(shape, index):
    return pl.BlockSpec(shape, lambda i: index, pipeline_mode=pl.Buffered(1))


def _norm_proj(x, g, w_stack, idx, tm):
    m = x.shape[0]
    n = w_stack.shape[2]
    return pl.pallas_call(
        _norm_proj_kernel,
        grid=(m // tm,),
        in_specs=[pl.BlockSpec((tm, D_MODEL), lambda i: (i, 0)),
                  _resident((1, D_MODEL), (0, 0)),
                  _resident((None, D_MODEL, n), (idx, 0, 0))],
        out_specs=pl.BlockSpec((tm, n), lambda i: (i, 0)),
        out_shape=jax.ShapeDtypeStruct((m, n), F32),
        compiler_params=_params("parallel"),
        name="norm_proj",
    )(x, g, w_stack)


def _out_ffn_kernel(x_ref, o_ref, wo_ref, g_ref, wg_ref, wu_ref, wd_ref, gf_ref, y_ref, *, final):
    xn = x_ref[...] + _dot(o_ref[...], wo_ref[...])
    hn = _rms(xn, g_ref[...]).astype(BF16)
    gate = _dot(hn, wg_ref[...])
    up = _dot(hn, wu_ref[...])
    y = xn + _dot(_silu(gate) * up, wd_ref[...])
    if final:
        y = _rms(y, gf_ref[...])
    y_ref[...] = y


def _out_ffn(x, o, w_out_stack, idx, g_ffn, w_up_stack, w_down_stack, layer, g_final, tm, final):
    m = x.shape[0]
    return pl.pallas_call(
        functools.partial(_out_ffn_kernel, final=final),
        grid=(m // tm,),
        in_specs=[pl.BlockSpec((tm, D_MODEL), lambda i: (i, 0)),
                  pl.BlockSpec((tm, D_MODEL), lambda i: (i, 0)),
                  _resident((None, D_MODEL, D_MODEL), (idx, 0, 0)),
                  _resident((1, D_MODEL), (0, 0)),
                  _resident((None, D_MODEL, D_FF), (layer, 0, 0)),
                  _resident((None, D_MODEL, D_FF), (layer, 0, 1)),
                  _resident((None, D_FF, D_MODEL), (layer, 0, 0)),
                  _resident((1, D_MODEL), (0, 0))],
        out_specs=pl.BlockSpec((tm, D_MODEL), lambda i: (i, 0)),
        out_shape=jax.ShapeDtypeStruct((m, D_MODEL), F32),
        compiler_params=_params("parallel"),
        name="out_ffn",
    )(x, o, w_out_stack, g_ffn, w_up_stack, w_up_stack, w_down_stack, g_final)


def _gla_logf(glr, wlr_ref, blr_ref):
    return _log_sigmoid(_dot_f32(glr, wlr_ref[...]) + blr_ref[...]) * (1.0 / GLA_TAU)


def _project_chunks(x, g_ref, w_ref, proj_ref):
    ps, c, _ = x.shape
    y = _dot(_rms(x.reshape(ps * c, D_MODEL), g_ref[...]), w_ref[...])
    proj_ref[...] = y.reshape(ps, c, D_PROJ)


def _even_prompt_kernel(xa_ref, xb_ref, g_ref, w_ref, cos_ref, sin_ref, wlr_ref, blr_ref, gng_ref,
                        gnr_ref, o_ref, sg_ref, sr_ref, p0_ref, p1_ref):
    c = EVEN_CHUNK

    @pl.when(pl.program_id(1) == 0)
    def _():
        sg_ref[...] = jnp.zeros_like(sg_ref)
        sr_ref[...] = jnp.zeros_like(sr_ref)
        _project_chunks(xa_ref[:, 0:c, :], g_ref, w_ref, p0_ref)

    def scan(p_ref, lo):
        for s in range(xa_ref.shape[0]):
            _even_chunk(p_ref.at[s], cos_ref.at[lo:lo + c], sin_ref.at[lo:lo + c], wlr_ref,
                        blr_ref, gng_ref, gnr_ref, o_ref.at[s, lo:lo + c], sg_ref.at[s],
                        sr_ref.at[s])

    scan(p0_ref, 0)
    _project_chunks(xa_ref[:, c:2 * c, :], g_ref, w_ref, p1_ref)
    scan(p1_ref, c)
    _project_chunks(xb_ref[...], g_ref, w_ref, p0_ref)


def _even_chunk(p_ref, cos_ref, sin_ref, wlr_ref, blr_ref, gng_ref, gnr_ref, o_ref, sg_ref, sr_ref):
    c = EVEN_CHUNK
    r_sub = GLA_SUB

    q = p_ref[:, 0:256] * DK_A ** -0.5
    k = p_ref[:, 256:512]
    v = p_ref[:, 512:1024]
    gg = p_ref[:, 1024:1536]
    logf = _gla_logf(p_ref[:, GATE_OFF:GATE_OFF + LANE], wlr_ref, blr_ref)
    b = _dot_f32(_tri(c), logf)
    s_gla = sg_ref[...]
    inter = _dot(_stack_heads(q * jnp.exp(b), H_A, DK_A), s_gla)
    o_a = jnp.concatenate([inter[h * c:(h + 1) * c] for h in range(H_A)], axis=1)

    ind = (_iota((H_A * DK_A, H_A * DV_A), 0) // DK_A ==
           _iota((H_A * DK_A, H_A * DV_A), 1) // DV_A).astype(BF16)
    row = _iota((r_sub, H_A * DK_A), 0)
    blocks = []
    for i in range(c // r_sub):
        lo = i * r_sub
        qb = q[lo:lo + r_sub]
        bb = b[lo:lo + r_sub]
        ws = []
        for s in range(r_sub):
            e = jnp.exp(jnp.minimum(bb - b[lo + s:lo + s + 1], 0.0))
            ws.append(jnp.where(row >= s, qb * k[lo + s:lo + s + 1] * e, 0.0))
        rep = _dot(jnp.concatenate(ws, axis=0), ind)
        ob = rep[0:r_sub] * v[lo:lo + 1]
        for s in range(1, r_sub):
            ob = ob + rep[s * r_sub:(s + 1) * r_sub] * v[lo + s:lo + s + 1]
        if i > 0:
            ref_b = b[lo - 1:lo]
            qi = qb * jnp.exp(bb - ref_b)
            ki = k[0:lo] * jnp.exp(ref_b - b[0:lo])
            sc = _dot_nt(_stack_heads(qi, H_A, DK_A), ki)
            ob = ob + _diag_lanes(_dot(sc, v[0:lo]), H_A, r_sub, DV_A)
        blocks.append(ob)
    o_a = o_a + jnp.concatenate(blocks, axis=0)
    o_a = _head_rmsnorm(o_a, H_A, DV_A) * gng_ref[...] * _silu(gg)

    b_t = b.T
    b_last = b_t[:, c - 1:c]
    kd_t = k.T * jnp.exp(b_last - b_t)
    upd = _diag_rows(_dot(kd_t, v), H_A, DK_A, DV_A)
    sg_ref[...] = s_gla * jnp.exp(b_last) + upd

    cos = cos_ref[...]
    sin = sin_ref[...]
    rq = _rope(p_ref[:, 1536:1792], cos, sin, DK_B) * DK_B ** -0.5
    rk = _rope(p_ref[:, 1792:2048], cos, sin, DK_B)
    rv = p_ref[:, 2048:2560]
    rg = p_ref[:, 2560:3072]
    s_ret = sr_ref[...]
    t_row = _iota((H_B * c, c), 0)
    lg_row = _head_select(t_row // c, LOG_GAMMA)
    t_loc = (t_row % c).astype(F32)
    s_loc = _iota((H_B * c, c), 1).astype(F32)
    decay = jnp.where(t_loc >= s_loc, jnp.exp(jnp.maximum(t_loc - s_loc, 0.0) * lg_row), 0.0)
    q_dec = jnp.exp((t_loc[:, 0:1] + 1.0) * lg_row[:, 0:1])
    qs = _stack_heads(rq, H_B, DK_B)
    scores = _dot_nt(qs, rk) * decay
    o_b = _diag_lanes(_dot(scores, rv), H_B, c, DV_B)
    inter_b = _dot(qs * q_dec, s_ret)
    o_b = o_b + jnp.concatenate([inter_b[h * c:(h + 1) * c] for h in range(H_B)], axis=1)
    o_b = _head_groupnorm(o_b, H_B, DV_B) * gnr_ref[...] * _silu(rg)

    lg_lane = _head_select(_iota((c, H_B * DK_B), 1) // DK_B, LOG_GAMMA)
    k_dec = jnp.exp((c - 1.0 - _iota((c, H_B * DK_B), 0).astype(F32)) * lg_lane)
    upd_b = _diag_rows(_dot((rk * k_dec).T, rv), H_B, DK_B, DV_B)
    g_chunk = jnp.exp(c * _head_select(_iota((H_B * DK_B, DV_B), 0) // DK_B, LOG_GAMMA))
    sr_ref[...] = s_ret * g_chunk + upd_b

    o_ref[...] = jnp.concatenate([o_a, o_b], axis=1).astype(BF16)


def _even_prompt(x, g_mix, w_stack, idx, cos, sin, w_lr, b_lr, gn_gla, gn_ret, nb, seq):
    c = EVEN_CHUNK
    ps = min(EVEN_SEQS, nb)
    steps = seq // (2 * c)
    last = seq // c - 1
    small = lambda shape: pl.BlockSpec(shape, lambda b, n: (0, 0))
    per_seq = lambda rows, cols: pl.BlockSpec((ps, rows, cols), lambda b, n: (b, 0, 0))
    x = x.reshape(nb, seq, D_MODEL)
    o, sg, sr = pl.pallas_call(
        _even_prompt_kernel,
        grid=(nb // ps, steps),
        in_specs=[pl.BlockSpec((ps, 2 * c, D_MODEL), lambda b, n: (b, n, 0)),
                  pl.BlockSpec((ps, c, D_MODEL),
                               lambda b, n: (b, jnp.minimum(2 * n + 2, last), 0)),
                  small((1, D_MODEL)),
                  pl.BlockSpec((None, D_MODEL, D_PROJ), lambda b, n: (idx, 0, 0),
                               pipeline_mode=pl.Buffered(1)),
                  pl.BlockSpec((2 * c, H_B * DK_B), lambda b, n: (n, 0)),
                  pl.BlockSpec((2 * c, H_B * DK_B), lambda b, n: (n, 0)),
                  small((LANE, H_A * DK_A)), small((1, H_A * DK_A)),
                  small((1, D_HALF)), small((1, D_HALF))],
        out_specs=[pl.BlockSpec((ps, 2 * c, D_MODEL), lambda b, n: (b, n, 0)),
                   per_seq(H_A * DK_A, DV_A), per_seq(H_B * DK_B, DV_B)],
        out_shape=[jax.ShapeDtypeStruct((nb, seq, D_MODEL), BF16),
                   jax.ShapeDtypeStruct((nb, H_A * DK_A, DV_A), F32),
                   jax.ShapeDtypeStruct((nb, H_B * DK_B, DV_B), F32)],
        scratch_shapes=[pltpu.VMEM((ps, c, D_PROJ), F32), pltpu.VMEM((ps, c, D_PROJ), F32)],
        compiler_params=_params("parallel", "arbitrary"),
        name="even_prompt",
    )(x, x, g_mix, w_stack, cos, sin, w_lr, b_lr, gn_gla, gn_ret)
    return o.reshape(nb * seq, D_MODEL), sg, sr


def _rg_gates(xc, wr_ref, br_ref, wi_ref, bi_ref, lam_ref):
    r = _sigmoid(_dot(xc, wr_ref[...]) + br_ref[...])
    i = _sigmoid(_dot(xc, wi_ref[...]) + bi_ref[...])
    log_a = -RG_C * r * _softplus(-lam_ref[...])
    a = jnp.exp(log_a)
    u = jnp.sqrt(jnp.tanh(-log_a) * (a * a + 1.0)) * (i * xc)
    return a, u


def _odd_prompt_kernel(xa_ref, xb_ref, g_ref, w_ref, big_ref, bfg_ref, gnm_ref, cw_ref, cb_ref,
                       wr_ref, br_ref, wi_ref, bi_ref, lam_ref, o_ref, cst_ref, nst_ref, mst_ref,
                       hst_ref, conv_ref, xpad_ref, p0_ref, p1_ref):
    c = ODD_CHUNK

    @pl.when(pl.program_id(1) == 0)
    def _():
        cst_ref[...] = jnp.zeros_like(cst_ref)
        nst_ref[...] = jnp.zeros_like(nst_ref)
        mst_ref[...] = jnp.zeros_like(mst_ref)
        hst_ref[...] = jnp.zeros_like(hst_ref)
        xpad_ref[:, 0:8, :] = jnp.zeros((PROMPT_SEQS, 8, D_RG), F32)
        _project_chunks(xa_ref[:, 0:c, :], g_ref, w_ref, p0_ref)

    def scan(p_ref, lo):
        for s in range(PROMPT_SEQS):
            _odd_chunk(p_ref.at[s], big_ref, bfg_ref, gnm_ref, cw_ref, cb_ref, wr_ref, br_ref,
                       wi_ref, bi_ref, lam_ref, o_ref.at[s, lo:lo + c], cst_ref.at[s],
                       nst_ref.at[s], mst_ref.at[s], hst_ref.at[s], conv_ref.at[s], xpad_ref.at[s])

    scan(p0_ref, 0)
    _project_chunks(xa_ref[:, c:2 * c, :], g_ref, w_ref, p1_ref)
    scan(p1_ref, c)
    _project_chunks(xb_ref[...], g_ref, w_ref, p0_ref)


def _odd_chunk(p_ref, big_ref, bfg_ref, gnm_ref, cw_ref, cb_ref, wr_ref, br_ref, wi_ref,
               bi_ref, lam_ref, o_ref, cst_ref, nst_ref, mst_ref, hst_ref, conv_ref, xpad_ref):
    c = ODD_CHUNK

    gates = p_ref[:, GATE_OFF:GATE_OFF + LANE]
    ig = gates + big_ref[...]
    lf = _log_sigmoid(gates + bfg_ref[...])
    b_all = pltpu.roll(_dot_f32(_tri(c), lf), LANE - H_C, 1)
    a_all = ig - b_all
    a_rows = a_all.T
    causal = _iota((c, c), 0) >= _iota((c, c), 1)
    m_prev = mst_ref[...]
    m_new = m_prev
    lane = _iota((1, LANE), 1)
    outs = []
    for h in range(H_C):
        sl = slice(h * DK_C, (h + 1) * DK_C)
        qh = p_ref[:, sl] * DK_C ** -0.5
        kh = p_ref[:, 512 + h * DK_C:512 + (h + 1) * DK_C]
        vh = p_ref[:, 1024 + h * DV_C:1024 + (h + 1) * DV_C]
        bcol = b_all[:, h:h + 1]
        acol = a_all[:, h:h + 1]
        arow = a_rows[h:h + 1, :]
        mp = m_prev[:, h:h + 1]
        cmax = jnp.max(jnp.where(causal, arow, -jnp.inf), axis=1, keepdims=True)
        m_t = jnp.maximum(bcol + mp, bcol + cmax)
        w_state = jnp.exp(bcol + mp - m_t)
        w_intra = jnp.where(causal, jnp.exp(jnp.minimum(bcol + arow - m_t, 0.0)), 0.0)
        c_h = cst_ref[sl, :]
        n_h = nst_ref[h:h + 1, :]
        scores = _dot_nt(qh, kh) * w_intra
        num = w_state * _dot(qh, c_h) + _dot(scores, vh)
        den = w_state * jnp.sum(qh * n_h, axis=1, keepdims=True) + \
            jnp.sum(scores, axis=1, keepdims=True)
        outs.append(num / jnp.maximum(jnp.abs(den), jnp.exp(-m_t)))
        m_last = m_t[c - 1:c]
        b_last = bcol[c - 1:c]
        w_s_last = jnp.exp(b_last + mp - m_last)
        kw = kh * jnp.exp(b_last + acol - m_last)
        cst_ref[sl, :] = c_h * w_s_last + _dot(kw.T, vh)
        nst_ref[h:h + 1, :] = n_h * w_s_last + jnp.sum(kw, axis=0, keepdims=True)
        m_new = jnp.where(lane == h, m_last, m_new)
    mst_ref[...] = m_new
    o_c = _head_rmsnorm(jnp.concatenate(outs, axis=1), H_C, DV_C) * gnm_ref[...] * \
        _sigmoid(p_ref[:, 1536:2048])

    rx = p_ref[:, 2048:2560]
    xpad_ref[8:8 + c, :] = rx
    xc = cb_ref[...] + cw_ref[3:4, :] * rx
    for j in range(1, CONV_W):
        xc = xc + cw_ref[3 - j:4 - j, :] * xpad_ref[8 - j:8 - j + c, :]
    conv_ref[...] = xpad_ref[c + 8 - (CONV_W - 1):c + 8, :]
    xpad_ref[0:8, :] = xpad_ref[c:c + 8, :]
    a, u = _rg_gates(xc, wr_ref, br_ref, wi_ref, bi_ref, lam_ref)
    a = a.reshape(c // SUBLANE, SUBLANE, D_RG)
    u = u.reshape(c // SUBLANE, SUBLANE, D_RG)
    t_idx = _iota(a.shape, 1)
    s = 1
    while s < SUBLANE:
        keep = t_idx >= s
        u = jnp.where(keep, a * pltpu.roll(u, s, 1) + u, u)
        a = jnp.where(keep, a * pltpu.roll(a, s, 1), a)
        s *= 2
    carry = hst_ref[...]
    groups = []
    for g in range(c // SUBLANE):
        hg = a[g] * carry + u[g]
        carry = hg[SUBLANE - 1:SUBLANE]
        groups.append(hg)
    hs = jnp.concatenate(groups, axis=0)
    hst_ref[...] = carry
    o_d = hs * _gelu_tanh(p_ref[:, 2560:3072])

    o_ref[...] = jnp.concatenate([o_c, o_d], axis=1).astype(BF16)


def _odd_prompt(x, g_mix, w_stack, idx, b_ig, b_fg, gn_ml, conv_w, conv_b, w_r, b_r, w_i, b_i, lam,
                nb, seq):
    c = ODD_CHUNK
    ps = PROMPT_SEQS
    steps = seq // (2 * c)
    last = seq // c - 1
    small = lambda shape: pl.BlockSpec(shape, lambda b, n: (0, 0))
    per_seq = lambda rows, cols: pl.BlockSpec((ps, rows, cols), lambda b, n: (b, 0, 0))
    x = x.reshape(nb, seq, D_MODEL)
    outs = pl.pallas_call(
        _odd_prompt_kernel,
        grid=(nb // ps, steps),
        in_specs=[pl.BlockSpec((ps, 2 * c, D_MODEL), lambda b, n: (b, n, 0)),
                  pl.BlockSpec((ps, c, D_MODEL),
                               lambda b, n: (b, jnp.minimum(2 * n + 2, last), 0)),
                  small((1, D_MODEL)),
                  pl.BlockSpec((None, D_MODEL, D_PROJ), lambda b, n: (idx, 0, 0),
                               pipeline_mode=pl.Buffered(1)),
                  small((1, LANE)), small((1, LANE)), small((1, D_HALF)),
                  small((CONV_W, D_RG)), small((1, D_RG)),
                  small((D_RG, D_RG)), small((1, D_RG)), small((D_RG, D_RG)), small((1, D_RG)),
                  small((1, D_RG))],
        out_specs=[pl.BlockSpec((ps, 2 * c, D_MODEL), lambda b, n: (b, n, 0)),
                   per_seq(H_C * DK_C, DV_C), per_seq(H_C, DK_C), per_seq(1, LANE),
                   per_seq(1, D_RG), per_seq(CONV_W - 1, D_RG)],
        out_shape=[jax.ShapeDtypeStruct((nb, seq, D_MODEL), BF16),
                   jax.ShapeDtypeStruct((nb, H_C * DK_C, DV_C), F32),
                   jax.ShapeDtypeStruct((nb, H_C, DK_C), F32),
                   jax.ShapeDtypeStruct((nb, 1, LANE), F32),
                   jax.ShapeDtypeStruct((nb, 1, D_RG), F32),
                   jax.ShapeDtypeStruct((nb, CONV_W - 1, D_RG), F32)],
        scratch_shapes=[pltpu.VMEM((ps, c + 8, D_RG), F32),
                        pltpu.VMEM((ps, c, D_PROJ), F32), pltpu.VMEM((ps, c, D_PROJ), F32)],
        compiler_params=_params("parallel", "arbitrary"),
        name="odd_prompt",
    )(x, x, g_mix, w_stack, b_ig, b_fg, gn_ml, conv_w, conv_b, w_r, b_r, w_i, b_i, lam)
    return (outs[0].reshape(nb * seq, D_MODEL),) + tuple(outs[1:])


def _rank1_step(s_in_ref, s_out_ref, o_scr_ref, q_t, k_t, d_t, v, n_heads, dk, dv):
    bb = v.shape[0]
    for j in range(bb):
        vst = jnp.concatenate(
            [jnp.broadcast_to(v[j:j + 1, h * dv:(h + 1) * dv], (dk, dv)) for h in range(n_heads)],
            axis=0)
        s_new = s_in_ref[j] * d_t[:, j:j + 1] + k_t[:, j:j + 1] * vst
        s_out_ref[j] = s_new
        prod = q_t[:, j:j + 1] * s_new
        o_scr_ref[j:j + 1, :] = jnp.concatenate(
            [jnp.sum(prod[h * dk:(h + 1) * dk], axis=0, keepdims=True) for h in range(n_heads)],
            axis=1)


def _even_step_kernel(*refs, n_prev):
    (p_ref, cos_ref, sin_ref, wlr_ref, blr_ref, gng_ref, gnr_ref, sg_in_ref, sr_in_ref) = refs[:9]
    o_ref, sg_out_ref, sr_out_ref, oa_ref, ob_ref = refs[9 + n_prev:]
    bb = STEP_BB
    q = p_ref[:, 0:256] * DK_A ** -0.5
    k = p_ref[:, 256:512]
    v = p_ref[:, 512:1024]
    gg = p_ref[:, 1024:1536]
    d = jnp.exp(_gla_logf(p_ref[:, GATE_OFF:GATE_OFF + LANE], wlr_ref, blr_ref))
    _rank1_step(sg_in_ref, sg_out_ref, oa_ref, q.T, k.T, d.T, v, H_A, DK_A, DV_A)
    o_a = _head_rmsnorm(oa_ref[...], H_A, DV_A) * gng_ref[...] * _silu(gg)

    cos = cos_ref[...]
    sin = sin_ref[...]
    rq = _rope(p_ref[:, 1536:1792], cos, sin, DK_B) * DK_B ** -0.5
    rk = _rope(p_ref[:, 1792:2048], cos, sin, DK_B)
    rv = p_ref[:, 2048:2560]
    rg = p_ref[:, 2560:3072]
    gamma_t = jnp.exp(_head_select(_iota((H_B * DK_B, bb), 0) // DK_B, LOG_GAMMA))
    _rank1_step(sr_in_ref, sr_out_ref, ob_ref, rq.T, rk.T, gamma_t, rv, H_B, DK_B, DV_B)
    o_b = _head_groupnorm(ob_ref[...], H_B, DV_B) * gnr_ref[...] * _silu(rg)
    o_ref[...] = jnp.concatenate([o_a, o_b], axis=1).astype(BF16)


def _even_step(proj, cos, sin, w_lr, b_lr, gn_gla, gn_ret, s_gla, s_ret, idx, prev):
    nb = proj.shape[0]
    bb = STEP_BB
    small = lambda shape: pl.BlockSpec(shape, lambda i: (0, 0))
    state = lambda rows, cols: pl.BlockSpec((None, bb, rows, cols), lambda i: (idx, i, 0, 0))
    n_in = 9
    return pl.pallas_call(
        functools.partial(_even_step_kernel, n_prev=len(prev)),
        grid=(nb // bb,),
        in_specs=[pl.BlockSpec((bb, D_PROJ), lambda i: (i, 0)),
                  small((1, H_B * DK_B)), small((1, H_B * DK_B)),
                  small((LANE, H_A * DK_A)), small((1, H_A * DK_A)),
                  small((1, D_HALF)), small((1, D_HALF)),
                  state(H_A * DK_A, DV_A), state(H_B * DK_B, DV_B)] +
                 [pl.BlockSpec(memory_space=pl.ANY)] * len(prev),
        out_specs=[pl.BlockSpec((bb, D_MODEL), lambda i: (i, 0)),
                   state(H_A * DK_A, DV_A), state(H_B * DK_B, DV_B)],
        out_shape=[jax.ShapeDtypeStruct((nb, D_MODEL), BF16),
                   jax.ShapeDtypeStruct(s_gla.shape, F32),
                   jax.ShapeDtypeStruct(s_ret.shape, F32)],
        input_output_aliases={n_in + j: 1 + j for j in range(len(prev))},
        scratch_shapes=[pltpu.VMEM((bb, D_HALF), F32), pltpu.VMEM((bb, D_HALF), F32)],
        compiler_params=_params("parallel"),
        name="even_step",
    )(proj, cos, sin, w_lr, b_lr, gn_gla, gn_ret, s_gla, s_ret, *prev)


def _odd_step_kernel(*refs, n_prev):
    (p_ref, big_ref, bfg_ref, gnm_ref, cw_ref, cb_ref, wr_ref, br_ref, wi_ref, bi_ref, lam_ref,
     c_in_ref, n_in_ref, m_in_ref, h_in_ref, conv_in_ref) = refs[:16]
    (o_ref, c_out_ref, n_out_ref, m_out_ref, h_out_ref, conv_out_ref, num_ref) = refs[16 + n_prev:]
    bb = STEP_BB
    gates = p_ref[:, GATE_OFF:GATE_OFF + LANE]
    ig = gates + big_ref[...]
    lf = pltpu.roll(_log_sigmoid(gates + bfg_ref[...]), LANE - H_C, 1)
    m_prev = m_in_ref[...]
    m_new = jnp.maximum(lf + m_prev, ig)
    valid = _iota((bb, LANE), 1) < H_C
    w_s = jnp.where(valid, jnp.exp(lf + m_prev - m_new), 0.0)
    w_i = jnp.where(valid, jnp.exp(ig - m_new), 0.0)
    m_out_ref[...] = jnp.where(valid, m_new, 0.0)
    expand = (_iota((LANE, H_C * DK_C), 0) == _iota((LANE, H_C * DK_C), 1) // DK_C).astype(F32)
    w_s_wide = _dot_f32(w_s, expand)
    w_i_wide = _dot_f32(w_i, expand)
    q = p_ref[:, 0:512] * DK_C ** -0.5
    kw = p_ref[:, 512:1024] * w_i_wide
    v = p_ref[:, 1024:1536]
    _rank1_step(c_in_ref, c_out_ref, num_ref, q.T, kw.T, w_s_wide.T, v, H_C, DK_C, DV_C)
    n_new = n_in_ref[...] * w_s_wide + kw
    n_out_ref[...] = n_new
    num = num_ref[...]
    qn = q * n_new
    outs = []
    for h in range(H_C):
        sl = slice(h * DK_C, (h + 1) * DK_C)
        den = jnp.sum(qn[:, sl], axis=1, keepdims=True)
        outs.append(num[:, sl] / jnp.maximum(jnp.abs(den), jnp.exp(-m_new[:, h:h + 1])))
    o_c = _head_rmsnorm(jnp.concatenate(outs, axis=1), H_C, DV_C) * gnm_ref[...] * \
        _sigmoid(p_ref[:, 1536:2048])

    rx = p_ref[:, 2048:2560]
    xc = cb_ref[...] + cw_ref[3:4, :] * rx
    for j in range(CONV_W - 1):
        xc = xc + cw_ref[j:j + 1, :] * conv_in_ref[j]
    for j in range(CONV_W - 2):
        conv_out_ref[j] = conv_in_ref[j + 1]
    conv_out_ref[CONV_W - 2] = rx
    a, u = _rg_gates(xc, wr_ref, br_ref, wi_ref, bi_ref, lam_ref)
    hs = a * h_in_ref[...] + u
    h_out_ref[...] = hs
    o_d = hs * _gelu_tanh(p_ref[:, 2560:3072])
    o_ref[...] = jnp.concatenate([o_c, o_d], axis=1).astype(BF16)


def _odd_step(proj, b_ig, b_fg, gn_ml, conv_w, conv_b, w_r, b_r, w_i, b_i, lam,
              s_c, s_n, s_m, s_h, s_conv, idx, prev):
    nb = proj.shape[0]
    bb = STEP_BB
    small = lambda shape: pl.BlockSpec(shape, lambda i: (0, 0))
    rows = lambda cols: pl.BlockSpec((bb, cols), lambda i: (i, 0))
    c_spec = pl.BlockSpec((None, bb, H_C * DK_C, DV_C), lambda i: (idx, i, 0, 0))
    conv_spec = pl.BlockSpec((CONV_W - 1, bb, D_RG), lambda i: (0, i, 0))
    n_in = 16
    return pl.pallas_call(
        functools.partial(_odd_step_kernel, n_prev=len(prev)),
        grid=(nb // bb,),
        in_specs=[rows(D_PROJ),
                  small((1, LANE)), small((1, LANE)), small((1, D_HALF)),
                  small((CONV_W, D_RG)), small((1, D_RG)),
                  small((D_RG, D_RG)), small((1, D_RG)), small((D_RG, D_RG)), small((1, D_RG)),
                  small((1, D_RG)),
                  c_spec, rows(H_C * DK_C), rows(LANE), rows(D_RG), conv_spec] +
                 [pl.BlockSpec(memory_space=pl.ANY)] * len(prev),
        out_specs=[rows(D_MODEL), c_spec, rows(H_C * DK_C), rows(LANE), rows(D_RG), conv_spec],
        out_shape=[jax.ShapeDtypeStruct((nb, D_MODEL), BF16),
                   jax.ShapeDtypeStruct(s_c.shape, F32),
                   jax.ShapeDtypeStruct(s_n.shape, F32),
                   jax.ShapeDtypeStruct(s_m.shape, F32),
                   jax.ShapeDtypeStruct(s_h.shape, F32),
                   jax.ShapeDtypeStruct(s_conv.shape, F32)],
        input_output_aliases={n_in + j: 1 + j for j in range(len(prev))},
        scratch_shapes=[pltpu.VMEM((bb, D_HALF), F32)],
        compiler_params=_params("parallel"),
        name="odd_step",
    )(proj, b_ig, b_fg, gn_ml, conv_w, conv_b, w_r, b_r, w_i, b_i, lam, s_c, s_n, s_m, s_h, s_conv,
      *prev)


def _rope_tables(pos):
    half = DK_B // 2
    inv = ROPE_BASE ** (-jnp.arange(half, dtype=F32) / half)
    ang = pos.astype(F32)[:, None] * inv[None, :]
    cos = jnp.tile(jnp.cos(ang), (1, 2 * H_B))
    sin = jnp.tile(jnp.concatenate([-jnp.sin(ang), jnp.sin(ang)], axis=1), (1, H_B))
    return cos, sin


PREP_ROWS = 256


def _reorder_cols_kernel(w_ref, o_ref, *, gate_lo, gate_hi):
    w = w_ref[...]
    pad = jnp.zeros((w.shape[0], LANE - (gate_hi - gate_lo)), F32)
    o_ref[:, 0:gate_lo] = w[:, 0:gate_lo].astype(BF16)
    o_ref[:, gate_lo:GATE_OFF] = w[:, gate_hi:].astype(BF16)
    o_ref[:, GATE_OFF:] = jnp.concatenate([w[:, gate_lo:gate_hi], pad], axis=1).astype(BF16)


def _reorder_cols(w, gate_lo, gate_hi):
    layers, rows, cols = w.shape
    return pl.pallas_call(
        functools.partial(_reorder_cols_kernel, gate_lo=gate_lo, gate_hi=gate_hi),
        grid=(layers, rows // PREP_ROWS),
        in_specs=[pl.BlockSpec((None, PREP_ROWS, cols), lambda l, r: (l, r, 0))],
        out_specs=pl.BlockSpec((None, PREP_ROWS, D_PROJ), lambda l, r: (l, r, 0)),
        out_shape=jax.ShapeDtypeStruct((layers, rows, D_PROJ), BF16),
        compiler_params=_params("parallel", "parallel"),
        name="reorder_cols",
    )(w)


def _pad_lanes(x, offset=0):
    return jnp.pad(x.reshape(1, -1), ((0, 0), (offset, LANE - offset - x.size)))


def _block_diag(w):
    return jax.scipy.linalg.block_diag(*[w[i] for i in range(RG_BLOCKS)]).astype(BF16)


def kernel(x_prompt, x_sample, state_gla, state_ret, state_mlstm_C, state_mlstm_n, state_mlstm_m, state_rglru_h, state_rglru_conv, norm_mix, norm_ffn, norm_final, even_w_in, even_w_lr, even_b_lr, even_gn_gla, even_gn_ret, even_w_out, odd_w_in, ml_b_i, ml_b_f, odd_gn_ml, rg_conv_w, rg_conv_b, rg_w_r, rg_b_r, rg_w_i, rg_b_i, rg_lam, odd_w_out, ffn_w_up, ffn_w_down):
    nb, seq, _ = x_prompt.shape
    ns = x_sample.shape[0]
    row = lambda x: x.reshape(1, -1)

    even_w_in_b = _reorder_cols(even_w_in, 1536, 1536 + GLA_RANK)
    odd_w_in_b = _reorder_cols(odd_w_in, 1536, 1536 + 2 * H_C)
    even_w_out_b = even_w_out.astype(BF16)
    odd_w_out_b = odd_w_out.astype(BF16)
    w_up_b = ffn_w_up.astype(BF16)
    w_down_b = ffn_w_down.astype(BF16)
    s_gla = state_gla.reshape(N_EVEN, ns, H_A * DK_A, DV_A)
    s_ret = state_ret.reshape(N_EVEN, ns, H_B * DK_B, DV_B)
    s_c = state_mlstm_C.reshape(N_ODD, ns, H_C * DK_C, DV_C)
    cos_p, sin_p = _rope_tables(jnp.arange(seq, dtype=jnp.int32))
    cos_s, sin_s = _rope_tables(PAST_LEN + jnp.arange(1, dtype=jnp.int32))

    xp = x_prompt.reshape(nb * seq, D_MODEL)
    xs = x_sample.reshape(ns, D_MODEL)
    st = {k: [] for k in ("gla_p", "ret_p", "c_p", "n_p", "m_p", "h_p", "conv_p",
                          "n_s", "m_s", "h_s", "conv_s")}
    even_new, c_new = [], []
    for layer in range(DEPTH):
        g_mix = row(norm_mix[layer])
        final = layer == DEPTH - 1
        if layer % 2 == 0:
            e = layer // 2
            w_in, w_out, idx = even_w_in_b, even_w_out_b, e
            w_lr = jnp.pad(even_w_lr[e], ((0, LANE - GLA_RANK), (0, 0)))
            small = (w_lr, row(even_b_lr[e]), row(even_gn_gla[e]), row(even_gn_ret[e]))
            o_p, sg, sr = _even_prompt(xp, g_mix, w_in, idx, cos_p, sin_p, *small, nb, seq)
            st["gla_p"].append(sg.reshape(nb, H_A, DK_A, DV_A))
            st["ret_p"].append(sr.reshape(nb, H_B, DK_B, DV_B))
            o_s, *even_new = _even_step(_norm_proj(xs, g_mix, w_in, idx, ns), cos_s, sin_s,
                                        *small, s_gla, s_ret, e, even_new)
        else:
            o = layer // 2
            w_in, w_out, idx = odd_w_in_b, odd_w_out_b, o
            small = (_pad_lanes(ml_b_i[o]), _pad_lanes(ml_b_f[o], H_C), row(odd_gn_ml[o]),
                     rg_conv_w[o], row(rg_conv_b[o]), _block_diag(rg_w_r[o]), row(rg_b_r[o]),
                     _block_diag(rg_w_i[o]), row(rg_b_i[o]), row(rg_lam[o]))
            o_p, c_, n_, m_, h_, cv = _odd_prompt(xp, g_mix, w_in, idx, *small, nb, seq)
            st["c_p"].append(c_.reshape(nb, H_C, DK_C, DV_C))
            st["n_p"].append(n_)
            st["m_p"].append(m_[:, 0, :H_C])
            st["h_p"].append(h_[:, 0, :])
            st["conv_p"].append(cv)
            o_s, c_, n_, m_, h_, cv = _odd_step(
                _norm_proj(xs, g_mix, w_in, idx, ns), *small,
                s_c,
                state_mlstm_n[o].reshape(ns, H_C * DK_C),
                jnp.pad(state_mlstm_m[o], ((0, 0), (0, LANE - H_C))),
                state_rglru_h[o],
                jnp.swapaxes(state_rglru_conv[o], 0, 1), o, c_new)
            c_new = [c_]
            st["n_s"].append(n_.reshape(ns, H_C, DK_C))
            st["m_s"].append(m_[:, :H_C])
            st["h_s"].append(h_)
            st["conv_s"].append(jnp.swapaxes(cv, 0, 1))
        ffn = (row(norm_ffn[layer]), w_up_b, w_down_b, layer, row(norm_final))
        xp = _out_ffn(xp, o_p, w_out, idx, *ffn, 512, final)
        xs = _out_ffn(xs, o_s, w_out, idx, *ffn, ns, final)

    stack = lambda name: jnp.stack(st[name])
    return (xp.reshape(nb, seq, D_MODEL), xs.reshape(ns, 1, D_MODEL),
            stack("gla_p"), stack("ret_p"), stack("c_p"), stack("n_p"), stack("m_p"),
            stack("h_p"), stack("conv_p"),
            even_new[0].reshape(state_gla.shape), even_new[1].reshape(state_ret.shape),
            c_new[0].reshape(state_mlstm_C.shape), stack("n_s"), stack("m_s"),
            stack("h_s"), stack("conv_s"))
```

```python
import functools

import numpy as np
import jax
import jax.numpy as jnp
from jax import lax
from jax.experimental import pallas as pl
from jax.experimental.pallas import tpu as pltpu

F32 = jnp.float32
BF16 = jnp.bfloat16

D_MODEL = 1024
DEPTH = 4
PAST_LEN = 16384
N_EVEN = (DEPTH + 1) // 2
N_ODD = DEPTH // 2
D_HALF = D_MODEL // 2
H_A = 4
DV_A = D_HALF // H_A
DK_A = DV_A // 2
GLA_RANK = 16
GLA_TAU = 16.0
H_B = 4
DV_B = D_HALF // H_B
DK_B = DV_B // 2
ROPE_BASE = 10000.0
H_C = 4
DK_C = D_HALF // H_C
DV_C = D_HALF // H_C
D_RG = D_HALF
RG_BLOCKS = 8
RG_BW = D_RG // RG_BLOCKS
RG_C = 8.0
CONV_W = 4
D_FF = (8 * D_MODEL + 3 * 256 - 1) // (3 * 256) * 256
EPS = 1e-6

LANE = 128
SUBLANE = 8
D_PROJ = 3072 + LANE
GATE_OFF = 3072
EVEN_CHUNK = 64
GLA_SUB = 16
ODD_CHUNK = 128
EVEN_SEQS = 4
PROMPT_SEQS = 4
STEP_BB = 8
VMEM_LIMIT = 56 * 1024 * 1024

LOG_GAMMA = [float(np.log1p(-np.exp2(-5.0 - h))) for h in range(H_B)]


def _dot(a, b):
    return jnp.dot(a.astype(BF16), b.astype(BF16), preferred_element_type=F32)


def _dot_nt(a, b):
    return lax.dot_general(a.astype(BF16), b.astype(BF16), (((1,), (1,)), ((), ())),
                           preferred_element_type=F32)


def _dot_f32(a, b):
    return jnp.dot(a, b, preferred_element_type=F32, precision=lax.Precision.HIGHEST)


def _rms(x, g):
    return x * lax.rsqrt(jnp.mean(x * x, axis=-1, keepdims=True) + EPS) * g


def _softplus(x):
    return jnp.maximum(x, 0.0) + jnp.log1p(jnp.exp(-jnp.abs(x)))


def _log_sigmoid(x):
    return -_softplus(-x)


def _sigmoid(x):
    return 0.5 * jnp.tanh(0.5 * x) + 0.5


def _silu(x):
    return x * _sigmoid(x)


def _gelu_tanh(x):
    return 0.5 * x * (1.0 + jnp.tanh(0.7978845608028654 * (x + 0.044715 * x * x * x)))


def _iota(shape, dim):
    return lax.broadcasted_iota(jnp.int32, shape, dim)


def _tri(c):
    return (_iota((c, c), 0) >= _iota((c, c), 1)).astype(F32)


def _stack_heads(x, n_heads, dk):
    head = _iota(x.shape, 1) // dk
    return jnp.concatenate([jnp.where(head == h, x, 0.0) for h in range(n_heads)], axis=0)


def _diag_lanes(y, n_heads, rows, cols):
    return jnp.concatenate(
        [y[h * rows:(h + 1) * rows, h * cols:(h + 1) * cols] for h in range(n_heads)], axis=1)


def _diag_rows(y, n_heads, rows, cols):
    return jnp.concatenate(
        [y[h * rows:(h + 1) * rows, h * cols:(h + 1) * cols] for h in range(n_heads)], axis=0)


def _head_select(idx, vals):
    out = jnp.full(idx.shape, vals[-1], F32)
    for h in range(len(vals) - 2, -1, -1):
        out = jnp.where(idx == h, vals[h], out)
    return out


def _swap_halves(x, dk):
    half = dk // 2
    parts = []
    for c in range(x.shape[1] // LANE):
        xs = x[:, c * LANE:(c + 1) * LANE]
        lo = _iota(xs.shape, 1) % dk < half
        parts.append(jnp.where(lo, pltpu.roll(xs, LANE - half, 1), pltpu.roll(xs, half, 1)))
    return jnp.concatenate(parts, axis=1)


def _rope(x, cos, sin_signed, dk):
    return x * cos + _swap_halves(x, dk) * sin_signed


def _head_rmsnorm(o, n_heads, d):
    parts = []
    for h in range(n_heads):
        oh = o[:, h * d:(h + 1) * d]
        parts.append(oh * lax.rsqrt(jnp.mean(oh * oh, axis=-1, keepdims=True) + EPS))
    return jnp.concatenate(parts, axis=1)


def _head_groupnorm(o, n_heads, d):
    parts = []
    for h in range(n_heads):
        oh = o[:, h * d:(h + 1) * d]
        c = oh - jnp.mean(oh, axis=-1, keepdims=True)
        parts.append(c * lax.rsqrt(jnp.mean(c * c, axis=-1, keepdims=True) + EPS))
    return jnp.concatenate(parts, axis=1)


def _params(*sem):
    return pltpu.CompilerParams(dimension_semantics=sem, vmem_limit_bytes=VMEM_LIMIT)


def _norm_proj_kernel(x_ref, g_ref, w_ref, o_ref):
    o_ref[...] = _dot(_rms(x_ref[...], g_ref[...]), w_ref[...])


def _resident(shape, index):
    return pl.BlockSpec(shape, lambda i: index, pipeline_mode=pl.Buffered(1))


def _norm_proj(x, g, w_stack, idx, tm):
    m = x.shape[0]
    n = w_stack.shape[2]
    return pl.pallas_call(
        _norm_proj_kernel,
        grid=(m // tm,),
        in_specs=[pl.BlockSpec((tm, D_MODEL), lambda i: (i, 0)),
                  _resident((1, D_MODEL), (0, 0)),
                  _resident((None, D_MODEL, n), (idx, 0, 0))],
        out_specs=pl.BlockSpec((tm, n), lambda i: (i, 0)),
        out_shape=jax.ShapeDtypeStruct((m, n), F32),
        compiler_params=_params("parallel"),
        name="norm_proj",
    )(x, g, w_stack)


def _out_ffn_kernel(x_ref, o_ref, wo_ref, g_ref, wg_ref, wu_ref, wd_ref, gf_ref, y_ref, *, final):
    xn = x_ref[...] + _dot(o_ref[...], wo_ref[...])
    hn = _rms(xn, g_ref[...]).astype(BF16)
    gate = _dot(hn, wg_ref[...])
    up = _dot(hn, wu_ref[...])
    y = xn + _dot(_silu(gate) * up, wd_ref[...])
    if final:
        y = _rms(y, gf_ref[...])
    y_ref[...] = y


def _out_ffn(x, o, w_out_stack, idx, g_ffn, w_up_stack, w_down_stack, layer, g_final, tm, final):
    m = x.shape[0]
    return pl.pallas_call(
        functools.partial(_out_ffn_kernel, final=final),
        grid=(m // tm,),
        in_specs=[pl.BlockSpec((tm, D_MODEL), lambda i: (i, 0)),
                  pl.BlockSpec((tm, D_MODEL), lambda i: (i, 0)),
                  _resident((None, D_MODEL, D_MODEL), (idx, 0, 0)),
                  _resident((1, D_MODEL), (0, 0)),
                  _resident((None, D_MODEL, D_FF), (layer, 0, 0)),
                  _resident((None, D_MODEL, D_FF), (layer, 0, 1)),
                  _resident((None, D_FF, D_MODEL), (layer, 0, 0)),
                  _resident((1, D_MODEL), (0, 0))],
        out_specs=pl.BlockSpec((tm, D_MODEL), lambda i: (i, 0)),
        out_shape=jax.ShapeDtypeStruct((m, D_MODEL), F32),
        compiler_params=_params("parallel"),
        name="out_ffn",
    )(x, o, w_out_stack, g_ffn, w_up_stack, w_up_stack, w_down_stack, g_final)


def _gla_logf(glr, wlr_ref, blr_ref):
    return _log_sigmoid(_dot_f32(glr, wlr_ref[...]) + blr_ref[...]) * (1.0 / GLA_TAU)


def _project_chunks(x, g_ref, w_ref, proj_ref):
    ps, c, _ = x.shape
    y = _dot(_rms(x.reshape(ps * c, D_MODEL), g_ref[...]), w_ref[...])
    proj_ref[...] = y.reshape(ps, c, D_PROJ)


def _even_prompt_kernel(xa_ref, xb_ref, g_ref, w_ref, cos_ref, sin_ref, wlr_ref, blr_ref, gng_ref,
                        gnr_ref, o_ref, sg_ref, sr_ref, p0_ref, p1_ref):
    c = EVEN_CHUNK

    @pl.when(pl.program_id(1) == 0)
    def _():
        sg_ref[...] = jnp.zeros_like(sg_ref)
        sr_ref[...] = jnp.zeros_like(sr_ref)
        _project_chunks(xa_ref[:, 0:c, :], g_ref, w_ref, p0_ref)

    def scan(p_ref, lo):
        for s in range(xa_ref.shape[0]):
            _even_chunk(p_ref.at[s], cos_ref.at[lo:lo + c], sin_ref.at[lo:lo + c], wlr_ref,
                        blr_ref, gng_ref, gnr_ref, o_ref.at[s, lo:lo + c], sg_ref.at[s],
                        sr_ref.at[s])

    scan(p0_ref, 0)
    _project_chunks(xa_ref[:, c:2 * c, :], g_ref, w_ref, p1_ref)
    scan(p1_ref, c)
    _project_chunks(xb_ref[...], g_ref, w_ref, p0_ref)


def _even_chunk(p_ref, cos_ref, sin_ref, wlr_ref, blr_ref, gng_ref, gnr_ref, o_ref, sg_ref, sr_ref):
    c = EVEN_CHUNK
    r_sub = GLA_SUB

    q = p_ref[:, 0:256] * DK_A ** -0.5
    k = p_ref[:, 256:512]
    v = p_ref[:, 512:1024]
    gg = p_ref[:, 1024:1536]
    logf = _gla_logf(p_ref[:, GATE_OFF:GATE_OFF + LANE], wlr_ref, blr_ref)
    b = _dot_f32(_tri(c), logf)
    s_gla = sg_ref[...]
    inter = _dot(_stack_heads(q * jnp.exp(b), H_A, DK_A), s_gla)
    o_a = jnp.concatenate([inter[h * c:(h + 1) * c] for h in range(H_A)], axis=1)

    ind = (_iota((H_A * DK_A, H_A * DV_A), 0) // DK_A ==
           _iota((H_A * DK_A, H_A * DV_A), 1) // DV_A).astype(BF16)
    row = _iota((r_sub, H_A * DK_A), 0)
    blocks = []
    for i in range(c // r_sub):
        lo = i * r_sub
        qb = q[lo:lo + r_sub]
        bb = b[lo:lo + r_sub]
        ws = []
        for s in range(r_sub):
            t0 = s // SUBLANE * SUBLANE
            rows_t = _iota((r_sub - t0, H_A * DK_A), 0) + t0
            e = jnp.exp(jnp.minimum(b[lo + t0:lo + r_sub] - b[lo + s:lo + s + 1], 0.0))
            ws.append(jnp.where(rows_t >= s, q[lo + t0:lo + r_sub] * k[lo + s:lo + s + 1] * e, 0.0))
        rep = _dot(jnp.concatenate(ws, axis=0), ind)
        groups = [None] * (r_sub // SUBLANE)
        off = 0
        for s in range(r_sub):
            t0 = s // SUBLANE * SUBLANE
            part = rep[off:off + r_sub - t0] * v[lo + s:lo + s + 1]
            off += r_sub - t0
            for g in range(t0 // SUBLANE, r_sub // SUBLANE):
                piece = part[g * SUBLANE - t0:(g + 1) * SUBLANE - t0]
                groups[g] = piece if groups[g] is None else groups[g] + piece
        ob = jnp.concatenate(groups, axis=0)
        if i > 0:
            ref_b = b[lo - 1:lo]
            qi = qb * jnp.exp(bb - ref_b)
            ki = k[0:lo] * jnp.exp(ref_b - b[0:lo])
            sc = _dot_nt(_stack_heads(qi, H_A, DK_A), ki)
            ob = ob + _diag_lanes(_dot(sc, v[0:lo]), H_A, r_sub, DV_A)
        blocks.append(ob)
    o_a = o_a + jnp.concatenate(blocks, axis=0)
    o_a = _head_rmsnorm(o_a, H_A, DV_A) * gng_ref[...] * _silu(gg)

    b_t = b.T
    b_last = b_t[:, c - 1:c]
    kd_t = k.T * jnp.exp(b_last - b_t)
    upd = _diag_rows(_dot(kd_t, v), H_A, DK_A, DV_A)
    sg_ref[...] = s_gla * jnp.exp(b_last) + upd

    cos = cos_ref[...]
    sin = sin_ref[...]
    rq = _rope(p_ref[:, 1536:1792], cos, sin, DK_B) * DK_B ** -0.5
    rk = _rope(p_ref[:, 1792:2048], cos, sin, DK_B)
    rv = p_ref[:, 2048:2560]
    rg = p_ref[:, 2560:3072]
    s_ret = sr_ref[...]
    t_row = _iota((H_B * c, c), 0)
    lg_row = _head_select(t_row // c, LOG_GAMMA)
    t_loc = (t_row % c).astype(F32)
    s_loc = _iota((H_B * c, c), 1).astype(F32)
    decay = jnp.where(t_loc >= s_loc, jnp.exp(jnp.maximum(t_loc - s_loc, 0.0) * lg_row), 0.0)
    q_dec = jnp.exp((t_loc[:, 0:1] + 1.0) * lg_row[:, 0:1])
    qs = _stack_heads(rq, H_B, DK_B)
    scores = _dot_nt(qs, rk) * decay
    o_b = _diag_lanes(_dot(scores, rv), H_B, c, DV_B)
    inter_b = _dot(qs * q_dec, s_ret)
    o_b = o_b + jnp.concatenate([inter_b[h * c:(h + 1) * c] for h in range(H_B)], axis=1)
    o_b = _head_groupnorm(o_b, H_B, DV_B) * gnr_ref[...] * _silu(rg)

    lg_lane = _head_select(_iota((c, H_B * DK_B), 1) // DK_B, LOG_GAMMA)
    k_dec = jnp.exp((c - 1.0 - _iota((c, H_B * DK_B), 0).astype(F32)) * lg_lane)
    upd_b = _diag_rows(_dot((rk * k_dec).T, rv), H_B, DK_B, DV_B)
    g_chunk = jnp.exp(c * _head_select(_iota((H_B * DK_B, DV_B), 0) // DK_B, LOG_GAMMA))
    sr_ref[...] = s_ret * g_chunk + upd_b

    o_ref[...] = jnp.concatenate([o_a, o_b], axis=1).astype(BF16)


def _even_prompt(x, g_mix, w_stack, idx, cos, sin, w_lr, b_lr, gn_gla, gn_ret, nb, seq):
    c = EVEN_CHUNK
    ps = min(EVEN_SEQS, nb)
    steps = seq // (2 * c)
    last = seq // c - 1
    small = lambda shape: pl.BlockSpec(shape, lambda b, n: (0, 0))
    per_seq = lambda rows, cols: pl.BlockSpec((ps, rows, cols), lambda b, n: (b, 0, 0))
    x = x.reshape(nb, seq, D_MODEL)
    o, sg, sr = pl.pallas_call(
        _even_prompt_kernel,
        grid=(nb // ps, steps),
        in_specs=[pl.BlockSpec((ps, 2 * c, D_MODEL), lambda b, n: (b, n, 0)),
                  pl.BlockSpec((ps, c, D_MODEL),
                               lambda b, n: (b, jnp.minimum(2 * n + 2, last), 0)),
                  small((1, D_MODEL)),
                  pl.BlockSpec((None, D_MODEL, D_PROJ), lambda b, n: (idx, 0, 0),
                               pipeline_mode=pl.Buffered(1)),
                  pl.BlockSpec((2 * c, H_B * DK_B), lambda b, n: (n, 0)),
                  pl.BlockSpec((2 * c, H_B * DK_B), lambda b, n: (n, 0)),
                  small((LANE, H_A * DK_A)), small((1, H_A * DK_A)),
                  small((1, D_HALF)), small((1, D_HALF))],
        out_specs=[pl.BlockSpec((ps, 2 * c, D_MODEL), lambda b, n: (b, n, 0)),
                   per_seq(H_A * DK_A, DV_A), per_seq(H_B * DK_B, DV_B)],
        out_shape=[jax.ShapeDtypeStruct((nb, seq, D_MODEL), BF16),
                   jax.ShapeDtypeStruct((nb, H_A * DK_A, DV_A), F32),
                   jax.ShapeDtypeStruct((nb, H_B * DK_B, DV_B), F32)],
        scratch_shapes=[pltpu.VMEM((ps, c, D_PROJ), F32), pltpu.VMEM((ps, c, D_PROJ), F32)],
        compiler_params=_params("parallel", "arbitrary"),
        name="even_prompt",
    )(x, x, g_mix, w_stack, cos, sin, w_lr, b_lr, gn_gla, gn_ret)
    return o.reshape(nb * seq, D_MODEL), sg, sr


def _rg_gates(xc, wr_ref, br_ref, wi_ref, bi_ref, lam_ref):
    r = _sigmoid(_dot(xc, wr_ref[...]) + br_ref[...])
    i = _sigmoid(_dot(xc, wi_ref[...]) + bi_ref[...])
    log_a = -RG_C * r * _softplus(-lam_ref[...])
    a = jnp.exp(log_a)
    u = jnp.sqrt(jnp.tanh(-log_a) * (a * a + 1.0)) * (i * xc)
    return a, u


def _odd_prompt_kernel(xa_ref, xb_ref, g_ref, w_ref, big_ref, bfg_ref, gnm_ref, cw_ref, cb_ref,
                       wr_ref, br_ref, wi_ref, bi_ref, lam_ref, o_ref, cst_ref, nst_ref, mst_ref,
                       hst_ref, conv_ref, xpad_ref, p0_ref, p1_ref):
    c = ODD_CHUNK

    @pl.when(pl.program_id(1) == 0)
    def _():
        cst_ref[...] = jnp.zeros_like(cst_ref)
        nst_ref[...] = jnp.zeros_like(nst_ref)
        mst_ref[...] = jnp.zeros_like(mst_ref)
        hst_ref[...] = jnp.zeros_like(hst_ref)
        xpad_ref[:, 0:8, :] = jnp.zeros((PROMPT_SEQS, 8, D_RG), F32)
        _project_chunks(xa_ref[:, 0:c, :], g_ref, w_ref, p0_ref)

    def scan(p_ref, lo):
        for s in range(PROMPT_SEQS):
            _odd_chunk(p_ref.at[s], big_ref, bfg_ref, gnm_ref, cw_ref, cb_ref, wr_ref, br_ref,
                       wi_ref, bi_ref, lam_ref, o_ref.at[s, lo:lo + c], cst_ref.at[s],
                       nst_ref.at[s], mst_ref.at[s], hst_ref.at[s], conv_ref.at[s], xpad_ref.at[s])

    scan(p0_ref, 0)
    _project_chunks(xa_ref[:, c:2 * c, :], g_ref, w_ref, p1_ref)
    scan(p1_ref, c)
    _project_chunks(xb_ref[...], g_ref, w_ref, p0_ref)


def _odd_chunk(p_ref, big_ref, bfg_ref, gnm_ref, cw_ref, cb_ref, wr_ref, br_ref, wi_ref,
               bi_ref, lam_ref, o_ref, cst_ref, nst_ref, mst_ref, hst_ref, conv_ref, xpad_ref):
    c = ODD_CHUNK

    gates = p_ref[:, GATE_OFF:GATE_OFF + LANE]
    ig = gates + big_ref[...]
    lf = _log_sigmoid(gates + bfg_ref[...])
    b_all = pltpu.roll(_dot_f32(_tri(c), lf), LANE - H_C, 1)
    a_all = ig - b_all
    a_rows = a_all.T
    causal = _iota((c, c), 0) >= _iota((c, c), 1)
    m_prev = mst_ref[...]
    m_new = m_prev
    lane = _iota((1, LANE), 1)
    outs = []
    for h in range(H_C):
        sl = slice(h * DK_C, (h + 1) * DK_C)
        qh = p_ref[:, sl] * DK_C ** -0.5
        kh = p_ref[:, 512 + h * DK_C:512 + (h + 1) * DK_C]
        vh = p_ref[:, 1024 + h * DV_C:1024 + (h + 1) * DV_C]
        bcol = b_all[:, h:h + 1]
        acol = a_all[:, h:h + 1]
        arow = a_rows[h:h + 1, :]
        mp = m_prev[:, h:h + 1]
        cmax = jnp.max(jnp.where(causal, arow, -jnp.inf), axis=1, keepdims=True)
        m_t = jnp.maximum(bcol + mp, bcol + cmax)
        w_state = jnp.exp(bcol + mp - m_t)
        w_intra = jnp.where(causal, jnp.exp(jnp.minimum(bcol + arow - m_t, 0.0)), 0.0)
        c_h = cst_ref[sl, :]
        n_h = nst_ref[h:h + 1, :]
        scores = _dot_nt(qh, kh) * w_intra
        num = w_state * _dot(qh, c_h) + _dot(scores, vh)
        den = w_state * jnp.sum(qh * n_h, axis=1, keepdims=True) + \
            jnp.sum(scores, axis=1, keepdims=True)
        outs.append(num / jnp.maximum(jnp.abs(den), jnp.exp(-m_t)))
        m_last = m_t[c - 1:c]
        b_last = bcol[c - 1:c]
        w_s_last = jnp.exp(b_last + mp - m_last)
        kw = kh * jnp.exp(b_last + acol - m_last)
        cst_ref[sl, :] = c_h * w_s_last + _dot(kw.T, vh)
        nst_ref[h:h + 1, :] = n_h * w_s_last + jnp.sum(kw, axis=0, keepdims=True)
        m_new = jnp.where(lane == h, m_last, m_new)
    mst_ref[...] = m_new
    o_c = _head_rmsnorm(jnp.concatenate(outs, axis=1), H_C, DV_C) * gnm_ref[...] * \
        _sigmoid(p_ref[:, 1536:2048])

    rx = p_ref[:, 2048:2560]
    xpad_ref[8:8 + c, :] = rx
    xc = cb_ref[...] + cw_ref[3:4, :] * rx
    for j in range(1, CONV_W):
        xc = xc + cw_ref[3 - j:4 - j, :] * xpad_ref[8 - j:8 - j + c, :]
    conv_ref[...] = xpad_ref[c + 8 - (CONV_W - 1):c + 8, :]
    xpad_ref[0:8, :] = xpad_ref[c:c + 8, :]
    a, u = _rg_gates(xc, wr_ref, br_ref, wi_ref, bi_ref, lam_ref)
    a = a.reshape(c // SUBLANE, SUBLANE, D_RG)
    u = u.reshape(c // SUBLANE, SUBLANE, D_RG)
    t_idx = _iota(a.shape, 1)
    s = 1
    while s < SUBLANE:
        keep = t_idx >= s
        u = jnp.where(keep, a * pltpu.roll(u, s, 1) + u, u)
        a = jnp.where(keep, a * pltpu.roll(a, s, 1), a)
        s *= 2
    carry = hst_ref[...]
    groups = []
    for g in range(c // SUBLANE):
        hg = a[g] * carry + u[g]
        carry = hg[SUBLANE - 1:SUBLANE]
        groups.append(hg)
    hs = jnp.concatenate(groups, axis=0)
    hst_ref[...] = carry
    o_d = hs * _gelu_tanh(p_ref[:, 2560:3072])

    o_ref[...] = jnp.concatenate([o_c, o_d], axis=1).astype(BF16)


def _odd_prompt(x, g_mix, w_stack, idx, b_ig, b_fg, gn_ml, conv_w, conv_b, w_r, b_r, w_i, b_i, lam,
                nb, seq):
    c = ODD_CHUNK
    ps = PROMPT_SEQS
    steps = seq // (2 * c)
    last = seq // c - 1
    small = lambda shape: pl.BlockSpec(shape, lambda b, n: (0, 0))
    per_seq = lambda rows, cols: pl.BlockSpec((ps, rows, cols), lambda b, n: (b, 0, 0))
    x = x.reshape(nb, seq, D_MODEL)
    outs = pl.pallas_call(
        _odd_prompt_kernel,
        grid=(nb // ps, steps),
        in_specs=[pl.BlockSpec((ps, 2 * c, D_MODEL), lambda b, n: (b, n, 0)),
                  pl.BlockSpec((ps, c, D_MODEL),
                               lambda b, n: (b, jnp.minimum(2 * n + 2, last), 0)),
                  small((1, D_MODEL)),
                  pl.BlockSpec((None, D_MODEL, D_PROJ), lambda b, n: (idx, 0, 0),
                               pipeline_mode=pl.Buffered(1)),
                  small((1, LANE)), small((1, LANE)), small((1, D_HALF)),
                  small((CONV_W, D_RG)), small((1, D_RG)),
                  small((D_RG, D_RG)), small((1, D_RG)), small((D_RG, D_RG)), small((1, D_RG)),
                  small((1, D_RG))],
        out_specs=[pl.BlockSpec((ps, 2 * c, D_MODEL), lambda b, n: (b, n, 0)),
                   per_seq(H_C * DK_C, DV_C), per_seq(H_C, DK_C), per_seq(1, LANE),
                   per_seq(1, D_RG), per_seq(CONV_W - 1, D_RG)],
        out_shape=[jax.ShapeDtypeStruct((nb, seq, D_MODEL), BF16),
                   jax.ShapeDtypeStruct((nb, H_C * DK_C, DV_C), F32),
                   jax.ShapeDtypeStruct((nb, H_C, DK_C), F32),
                   jax.ShapeDtypeStruct((nb, 1, LANE), F32),
                   jax.ShapeDtypeStruct((nb, 1, D_RG), F32),
                   jax.ShapeDtypeStruct((nb, CONV_W - 1, D_RG), F32)],
        scratch_shapes=[pltpu.VMEM((ps, c + 8, D_RG), F32),
                        pltpu.VMEM((ps, c, D_PROJ), F32), pltpu.VMEM((ps, c, D_PROJ), F32)],
        compiler_params=_params("parallel", "arbitrary"),
        name="odd_prompt",
    )(x, x, g_mix, w_stack, b_ig, b_fg, gn_ml, conv_w, conv_b, w_r, b_r, w_i, b_i, lam)
    return (outs[0].reshape(nb * seq, D_MODEL),) + tuple(outs[1:])


def _rank1_step(s_in_ref, s_out_ref, o_scr_ref, q_t, k_t, d_t, v, n_heads, dk, dv):
    bb = v.shape[0]
    for j in range(bb):
        vst = jnp.concatenate(
            [jnp.broadcast_to(v[j:j + 1, h * dv:(h + 1) * dv], (dk, dv)) for h in range(n_heads)],
            axis=0)
        s_new = s_in_ref[j] * d_t[:, j:j + 1] + k_t[:, j:j + 1] * vst
        s_out_ref[j] = s_new
        prod = q_t[:, j:j + 1] * s_new
        o_scr_ref[j:j + 1, :] = jnp.concatenate(
            [jnp.sum(prod[h * dk:(h + 1) * dk], axis=0, keepdims=True) for h in range(n_heads)],
            axis=1)


def _even_step_kernel(*refs, n_prev):
    (p_ref, cos_ref, sin_ref, wlr_ref, blr_ref, gng_ref, gnr_ref, sg_in_ref, sr_in_ref) = refs[:9]
    o_ref, sg_out_ref, sr_out_ref, oa_ref, ob_ref = refs[9 + n_prev:]
    bb = STEP_BB
    q = p_ref[:, 0:256] * DK_A ** -0.5
    k = p_ref[:, 256:512]
    v = p_ref[:, 512:1024]
    gg = p_ref[:, 1024:1536]
    d = jnp.exp(_gla_logf(p_ref[:, GATE_OFF:GATE_OFF + LANE], wlr_ref, blr_ref))
    _rank1_step(sg_in_ref, sg_out_ref, oa_ref, q.T, k.T, d.T, v, H_A, DK_A, DV_A)
    o_a = _head_rmsnorm(oa_ref[...], H_A, DV_A) * gng_ref[...] * _silu(gg)

    cos = cos_ref[...]
    sin = sin_ref[...]
    rq = _rope(p_ref[:, 1536:1792], cos, sin, DK_B) * DK_B ** -0.5
    rk = _rope(p_ref[:, 1792:2048], cos, sin, DK_B)
    rv = p_ref[:, 2048:2560]
    rg = p_ref[:, 2560:3072]
    gamma_t = jnp.exp(_head_select(_iota((H_B * DK_B, bb), 0) // DK_B, LOG_GAMMA))
    _rank1_step(sr_in_ref, sr_out_ref, ob_ref, rq.T, rk.T, gamma_t, rv, H_B, DK_B, DV_B)
    o_b = _head_groupnorm(ob_ref[...], H_B, DV_B) * gnr_ref[...] * _silu(rg)
    o_ref[...] = jnp.concatenate([o_a, o_b], axis=1).astype(BF16)


def _even_step(proj, cos, sin, w_lr, b_lr, gn_gla, gn_ret, s_gla, s_ret, idx, prev):
    nb = proj.shape[0]
    bb = STEP_BB
    small = lambda shape: pl.BlockSpec(shape, lambda i: (0, 0))
    state = lambda rows, cols: pl.BlockSpec((None, bb, rows, cols), lambda i: (idx, i, 0, 0))
    n_in = 9
    return pl.pallas_call(
        functools.partial(_even_step_kernel, n_prev=len(prev)),
        grid=(nb // bb,),
        in_specs=[pl.BlockSpec((bb, D_PROJ), lambda i: (i, 0)),
                  small((1, H_B * DK_B)), small((1, H_B * DK_B)),
                  small((LANE, H_A * DK_A)), small((1, H_A * DK_A)),
                  small((1, D_HALF)), small((1, D_HALF)),
                  state(H_A * DK_A, DV_A), state(H_B * DK_B, DV_B)] +
                 [pl.BlockSpec(memory_space=pl.ANY)] * len(prev),
        out_specs=[pl.BlockSpec((bb, D_MODEL), lambda i: (i, 0)),
                   state(H_A * DK_A, DV_A), state(H_B * DK_B, DV_B)],
        out_shape=[jax.ShapeDtypeStruct((nb, D_MODEL), BF16),
                   jax.ShapeDtypeStruct(s_gla.shape, F32),
                   jax.ShapeDtypeStruct(s_ret.shape, F32)],
        input_output_aliases={n_in + j: 1 + j for j in range(len(prev))},
        scratch_shapes=[pltpu.VMEM((bb, D_HALF), F32), pltpu.VMEM((bb, D_HALF), F32)],
        compiler_params=_params("parallel"),
        name="even_step",
    )(proj, cos, sin, w_lr, b_lr, gn_gla, gn_ret, s_gla, s_ret, *prev)


def _odd_step_kernel(*refs, n_prev):
    (p_ref, big_ref, bfg_ref, gnm_ref, cw_ref, cb_ref, wr_ref, br_ref, wi_ref, bi_ref, lam_ref,
     c_in_ref, n_in_ref, m_in_ref, h_in_ref, conv_in_ref) = refs[:16]
    (o_ref, c_out_ref, n_out_ref, m_out_ref, h_out_ref, conv_out_ref, num_ref) = refs[16 + n_prev:]
    bb = STEP_BB
    gates = p_ref[:, GATE_OFF:GATE_OFF + LANE]
    ig = gates + big_ref[...]
    lf = pltpu.roll(_log_sigmoid(gates + bfg_ref[...]), LANE - H_C, 1)
    m_prev = m_in_ref[...]
    m_new = jnp.maximum(lf + m_prev, ig)
    valid = _iota((bb, LANE), 1) < H_C
    w_s = jnp.where(valid, jnp.exp(lf + m_prev - m_new), 0.0)
    w_i = jnp.where(valid, jnp.exp(ig - m_new), 0.0)
    m_out_ref[...] = jnp.where(valid, m_new, 0.0)
    expand = (_iota((LANE, H_C * DK_C), 0) == _iota((LANE, H_C * DK_C), 1) // DK_C).astype(F32)
    w_s_wide = _dot_f32(w_s, expand)
    w_i_wide = _dot_f32(w_i, expand)
    q = p_ref[:, 0:512] * DK_C ** -0.5
    kw = p_ref[:, 512:1024] * w_i_wide
    v = p_ref[:, 1024:1536]
    _rank1_step(c_in_ref, c_out_ref, num_ref, q.T, kw.T, w_s_wide.T, v, H_C, DK_C, DV_C)
    n_new = n_in_ref[...] * w_s_wide + kw
    n_out_ref[...] = n_new
    num = num_ref[...]
    qn = q * n_new
    outs = []
    for h in range(H_C):
        sl = slice(h * DK_C, (h + 1) * DK_C)
        den = jnp.sum(qn[:, sl], axis=1, keepdims=True)
        outs.append(num[:, sl] / jnp.maximum(jnp.abs(den), jnp.exp(-m_new[:, h:h + 1])))
    o_c = _head_rmsnorm(jnp.concatenate(outs, axis=1), H_C, DV_C) * gnm_ref[...] * \
        _sigmoid(p_ref[:, 1536:2048])

    rx = p_ref[:, 2048:2560]
    xc = cb_ref[...] + cw_ref[3:4, :] * rx
    for j in range(CONV_W - 1):
        xc = xc + cw_ref[j:j + 1, :] * conv_in_ref[j]
    for j in range(CONV_W - 2):
        conv_out_ref[j] = conv_in_ref[j + 1]
    conv_out_ref[CONV_W - 2] = rx
    a, u = _rg_gates(xc, wr_ref, br_ref, wi_ref, bi_ref, lam_ref)
    hs = a * h_in_ref[...] + u
    h_out_ref[...] = hs
    o_d = hs * _gelu_tanh(p_ref[:, 2560:3072])
    o_ref[...] = jnp.concatenate([o_c, o_d], axis=1).astype(BF16)


def _odd_step(proj, b_ig, b_fg, gn_ml, conv_w, conv_b, w_r, b_r, w_i, b_i, lam,
              s_c, s_n, s_m, s_h, s_conv, idx, prev):
    nb = proj.shape[0]
    bb = STEP_BB
    small = lambda shape: pl.BlockSpec(shape, lambda i: (0, 0))
    rows = lambda cols: pl.BlockSpec((bb, cols), lambda i: (i, 0))
    c_spec = pl.BlockSpec((None, bb, H_C * DK_C, DV_C), lambda i: (idx, i, 0, 0))
    conv_spec = pl.BlockSpec((CONV_W - 1, bb, D_RG), lambda i: (0, i, 0))
    n_in = 16
    return pl.pallas_call(
        functools.partial(_odd_step_kernel, n_prev=len(prev)),
        grid=(nb // bb,),
        in_specs=[rows(D_PROJ),
                  small((1, LANE)), small((1, LANE)), small((1, D_HALF)),
                  small((CONV_W, D_RG)), small((1, D_RG)),
                  small((D_RG, D_RG)), small((1, D_RG)), small((D_RG, D_RG)), small((1, D_RG)),
                  small((1, D_RG)),
                  c_spec, rows(H_C * DK_C), rows(LANE), rows(D_RG), conv_spec] +
                 [pl.BlockSpec(memory_space=pl.ANY)] * len(prev),
        out_specs=[rows(D_MODEL), c_spec, rows(H_C * DK_C), rows(LANE), rows(D_RG), conv_spec],
        out_shape=[jax.ShapeDtypeStruct((nb, D_MODEL), BF16),
                   jax.ShapeDtypeStruct(s_c.shape, F32),
                   jax.ShapeDtypeStruct(s_n.shape, F32),
                   jax.ShapeDtypeStruct(s_m.shape, F32),
                   jax.ShapeDtypeStruct(s_h.shape, F32),
                   jax.ShapeDtypeStruct(s_conv.shape, F32)],
        input_output_aliases={n_in + j: 1 + j for j in range(len(prev))},
        scratch_shapes=[pltpu.VMEM((bb, D_HALF), F32)],
        compiler_params=_params("parallel"),
        name="odd_step",
    )(proj, b_ig, b_fg, gn_ml, conv_w, conv_b, w_r, b_r, w_i, b_i, lam, s_c, s_n, s_m, s_h, s_conv,
      *prev)


def _rope_tables(pos):
    half = DK_B // 2
    inv = ROPE_BASE ** (-jnp.arange(half, dtype=F32) / half)
    ang = pos.astype(F32)[:, None] * inv[None, :]
    cos = jnp.tile(jnp.cos(ang), (1, 2 * H_B))
    sin = jnp.tile(jnp.concatenate([-jnp.sin(ang), jnp.sin(ang)], axis=1), (1, H_B))
    return cos, sin


PREP_ROWS = 256


def _reorder_cols_kernel(w_ref, o_ref, *, gate_lo, gate_hi):
    w = w_ref[...]
    pad = jnp.zeros((w.shape[0], LANE - (gate_hi - gate_lo)), F32)
    o_ref[:, 0:gate_lo] = w[:, 0:gate_lo].astype(BF16)
    o_ref[:, gate_lo:GATE_OFF] = w[:, gate_hi:].astype(BF16)
    o_ref[:, GATE_OFF:] = jnp.concatenate([w[:, gate_lo:gate_hi], pad], axis=1).astype(BF16)


def _reorder_cols(w, gate_lo, gate_hi):
    layers, rows, cols = w.shape
    return pl.pallas_call(
        functools.partial(_reorder_cols_kernel, gate_lo=gate_lo, gate_hi=gate_hi),
        grid=(layers, rows // PREP_ROWS),
        in_specs=[pl.BlockSpec((None, PREP_ROWS, cols), lambda l, r: (l, r, 0))],
        out_specs=pl.BlockSpec((None, PREP_ROWS, D_PROJ), lambda l, r: (l, r, 0)),
        out_shape=jax.ShapeDtypeStruct((layers, rows, D_PROJ), BF16),
        compiler_params=_params("parallel", "parallel"),
        name="reorder_cols",
    )(w)


def _pad_lanes(x, offset=0):
    return jnp.pad(x.reshape(1, -1), ((0, 0), (offset, LANE - offset - x.size)))


def _block_diag(w):
    return jax.scipy.linalg.block_diag(*[w[i] for i in range(RG_BLOCKS)]).astype(BF16)


def kernel(x_prompt, x_sample, state_gla, state_ret, state_mlstm_C, state_mlstm_n, state_mlstm_m, state_rglru_h, state_rglru_conv, norm_mix, norm_ffn, norm_final, even_w_in, even_w_lr, even_b_lr, even_gn_gla, even_gn_ret, even_w_out, odd_w_in, ml_b_i, ml_b_f, odd_gn_ml, rg_conv_w, rg_conv_b, rg_w_r, rg_b_r, rg_w_i, rg_b_i, rg_lam, odd_w_out, ffn_w_up, ffn_w_down):
    nb, seq, _ = x_prompt.shape
    ns = x_sample.shape[0]
    row = lambda x: x.reshape(1, -1)

    even_w_in_b = _reorder_cols(even_w_in, 1536, 1536 + GLA_RANK)
    odd_w_in_b = _reorder_cols(odd_w_in, 1536, 1536 + 2 * H_C)
    even_w_out_b = even_w_out.astype(BF16)
    odd_w_out_b = odd_w_out.astype(BF16)
    w_up_b = ffn_w_up.astype(BF16)
    w_down_b = ffn_w_down.astype(BF16)
    s_gla = state_gla.reshape(N_EVEN, ns, H_A * DK_A, DV_A)
    s_ret = state_ret.reshape(N_EVEN, ns, H_B * DK_B, DV_B)
    s_c = state_mlstm_C.reshape(N_ODD, ns, H_C * DK_C, DV_C)
    cos_p, sin_p = _rope_tables(jnp.arange(seq, dtype=jnp.int32))
    cos_s, sin_s = _rope_tables(PAST_LEN + jnp.arange(1, dtype=jnp.int32))

    xp = x_prompt.reshape(nb * seq, D_MODEL)
    xs = x_sample.reshape(ns, D_MODEL)
    st = {k: [] for k in ("gla_p", "ret_p", "c_p", "n_p", "m_p", "h_p", "conv_p",
                          "n_s", "m_s", "h_s", "conv_s")}
    even_new, c_new = [], []
    for layer in range(DEPTH):
        g_mix = row(norm_mix[layer])
        final = layer == DEPTH - 1
        if layer % 2 == 0:
            e = layer // 2
            w_in, w_out, idx = even_w_in_b, even_w_out_b, e
            w_lr = jnp.pad(even_w_lr[e], ((0, LANE - GLA_RANK), (0, 0)))
            small = (w_lr, row(even_b_lr[e]), row(even_gn_gla[e]), row(even_gn_ret[e]))
            o_p, sg, sr = _even_prompt(xp, g_mix, w_in, idx, cos_p, sin_p, *small, nb, seq)
            st["gla_p"].append(sg.reshape(nb, H_A, DK_A, DV_A))
            st["ret_p"].append(sr.reshape(nb, H_B, DK_B, DV_B))
            o_s, *even_new = _even_step(_norm_proj(xs, g_mix, w_in, idx, ns), cos_s, sin_s,
                                        *small, s_gla, s_ret, e, even_new)
        else:
            o = layer // 2
            w_in, w_out, idx = odd_w_in_b, odd_w_out_b, o
            small = (_pad_lanes(ml_b_i[o]), _pad_lanes(ml_b_f[o], H_C), row(odd_gn_ml[o]),
                     rg_conv_w[o], row(rg_conv_b[o]), _block_diag(rg_w_r[o]), row(rg_b_r[o]),
                     _block_diag(rg_w_i[o]), row(rg_b_i[o]), row(rg_lam[o]))
            o_p, c_, n_, m_, h_, cv = _odd_prompt(xp, g_mix, w_in, idx, *small, nb, seq)
            st["c_p"].append(c_.reshape(nb, H_C, DK_C, DV_C))
            st["n_p"].append(n_)
            st["m_p"].append(m_[:, 0, :H_C])
            st["h_p"].append(h_[:, 0, :])
            st["conv_p"].append(cv)
            o_s, c_, n_, m_, h_, cv = _odd_step(
                _norm_proj(xs, g_mix, w_in, idx, ns), *small,
                s_c,
                state_mlstm_n[o].reshape(ns, H_C * DK_C),
                jnp.pad(state_mlstm_m[o], ((0, 0), (0, LANE - H_C))),
                state_rglru_h[o],
                jnp.swapaxes(state_rglru_conv[o], 0, 1), o, c_new)
            c_new = [c_]
            st["n_s"].append(n_.reshape(ns, H_C, DK_C))
            st["m_s"].append(m_[:, :H_C])
            st["h_s"].append(h_)
            st["conv_s"].append(jnp.swapaxes(cv, 0, 1))
        ffn = (row(norm_ffn[layer]), w_up_b, w_down_b, layer, row(norm_final))
        xp = _out_ffn(xp, o_p, w_out, idx, *ffn, 512, final)
        xs = _out_ffn(xs, o_s, w_out, idx, *ffn, ns, final)

    stack = lambda name: jnp.stack(st[name])
    return (xp.reshape(nb, seq, D_MODEL), xs.reshape(ns, 1, D_MODEL),
            stack("gla_p"), stack("ret_p"), stack("c_p"), stack("n_p"), stack("m_p"),
            stack("h_p"), stack("conv_p"),
            even_new[0].reshape(state_gla.shape), even_new[1].reshape(state_ret.shape),
            c_new[0].reshape(state_mlstm_C.shape), stack("n_s"), stack("m_s"),
            stack("h_s"), stack("conv_s"))
```

```python
import functools

import numpy as np
import jax
import jax.numpy as jnp
from jax import lax
from jax.experimental import pallas as pl
from jax.experimental.pallas import tpu as pltpu

F32 = jnp.float32
BF16 = jnp.bfloat16

D_MODEL = 1024
DEPTH = 4
PAST_LEN = 16384
N_EVEN = (DEPTH + 1) // 2
N_ODD = DEPTH // 2
D_HALF = D_MODEL // 2
H_A = 4
DV_A = D_HALF // H_A
DK_A = DV_A // 2
GLA_RANK = 16
GLA_TAU = 16.0
H_B = 4
DV_B = D_HALF // H_B
DK_B = DV_B // 2
ROPE_BASE = 10000.0
H_C = 4
DK_C = D_HALF // H_C
DV_C = D_HALF // H_C
D_RG = D_HALF
RG_BLOCKS = 8
RG_BW = D_RG // RG_BLOCKS
RG_C = 8.0
CONV_W = 4
D_FF = (8 * D_MODEL + 3 * 256 - 1) // (3 * 256) * 256
EPS = 1e-6

LANE = 128
SUBLANE = 8
D_PROJ = 3072 + LANE
GATE_OFF = 3072
EVEN_CHUNK = 64
GLA_SUB = 16
ODD_CHUNK = 128
EVEN_SEQS = 4
PROMPT_SEQS = 4
STEP_BB = 8
VMEM_LIMIT = 56 * 1024 * 1024

LOG_GAMMA = [float(np.log1p(-np.exp2(-5.0 - h))) for h in range(H_B)]


def _dot(a, b):
    return jnp.dot(a.astype(BF16), b.astype(BF16), preferred_element_type=F32)


def _dot_nt(a, b):
    return lax.dot_general(a.astype(BF16), b.astype(BF16), (((1,), (1,)), ((), ())),
                           preferred_element_type=F32)


def _dot_f32(a, b):
    return jnp.dot(a, b, preferred_element_type=F32, precision=lax.Precision.HIGHEST)


def _cumsum_rows(x):
    tri = _tri(x.shape[0]).astype(BF16)
    hi = x.astype(BF16)
    rest = x - hi.astype(F32)
    mid = rest.astype(BF16)
    lo = (rest - mid.astype(F32)).astype(BF16)
    return (jnp.dot(tri, hi, preferred_element_type=F32) +
            jnp.dot(tri, mid, preferred_element_type=F32) +
            jnp.dot(tri, lo, preferred_element_type=F32))


def _rms(x, g):
    return x * lax.rsqrt(jnp.mean(x * x, axis=-1, keepdims=True) + EPS) * g


def _softplus(x):
    return jnp.maximum(x, 0.0) + jnp.log1p(jnp.exp(-jnp.abs(x)))


def _log_sigmoid(x):
    return -_softplus(-x)


def _sigmoid(x):
    return 0.5 * jnp.tanh(0.5 * x) + 0.5


def _silu(x):
    return x * _sigmoid(x)


def _gelu_tanh(x):
    return 0.5 * x * (1.0 + jnp.tanh(0.7978845608028654 * (x + 0.044715 * x * x * x)))


def _iota(shape, dim):
    return lax.broadcasted_iota(jnp.int32, shape, dim)


def _tri(c):
    return (_iota((c, c), 0) >= _iota((c, c), 1)).astype(F32)


def _stack_heads(x, n_heads, dk):
    head = _iota(x.shape, 1) // dk
    return jnp.concatenate([jnp.where(head == h, x, 0.0) for h in range(n_heads)], axis=0)


def _diag_lanes(y, n_heads, rows, cols):
    return jnp.concatenate(
        [y[h * rows:(h + 1) * rows, h * cols:(h + 1) * cols] for h in range(n_heads)], axis=1)


def _head_dots_rows(a_t, b, n_heads, rows, cols):
    return jnp.concatenate(
        [_dot(a_t[h * rows:(h + 1) * rows], b[:, h * cols:(h + 1) * cols]) for h in range(n_heads)],
        axis=0)


def _head_select(idx, vals):
    out = jnp.full(idx.shape, vals[-1], F32)
    for h in range(len(vals) - 2, -1, -1):
        out = jnp.where(idx == h, vals[h], out)
    return out


def _swap_halves(x, dk):
    half = dk // 2
    parts = []
    for c in range(x.shape[1] // LANE):
        xs = x[:, c * LANE:(c + 1) * LANE]
        lo = _iota(xs.shape, 1) % dk < half
        parts.append(jnp.where(lo, pltpu.roll(xs, LANE - half, 1), pltpu.roll(xs, half, 1)))
    return jnp.concatenate(parts, axis=1)


def _rope(x, cos, sin_signed, dk):
    return x * cos + _swap_halves(x, dk) * sin_signed


def _head_rmsnorm(o, n_heads, d):
    parts = []
    for h in range(n_heads):
        oh = o[:, h * d:(h + 1) * d]
        parts.append(oh * lax.rsqrt(jnp.mean(oh * oh, axis=-1, keepdims=True) + EPS))
    return jnp.concatenate(parts, axis=1)


def _head_groupnorm(o, n_heads, d):
    parts = []
    for h in range(n_heads):
        oh = o[:, h * d:(h + 1) * d]
        c = oh - jnp.mean(oh, axis=-1, keepdims=True)
        parts.append(c * lax.rsqrt(jnp.mean(c * c, axis=-1, keepdims=True) + EPS))
    return jnp.concatenate(parts, axis=1)


def _params(*sem):
    return pltpu.CompilerParams(dimension_semantics=sem, vmem_limit_bytes=VMEM_LIMIT)


def _norm_proj_kernel(x_ref, g_ref, w_ref, o_ref):
    o_ref[...] = _dot(_rms(x_ref[...], g_ref[...]), w_ref[...])


def _resident(shape, index):
    return pl.BlockSpec(shape, lambda i: index, pipeline_mode=pl.Buffered(1))


def _norm_proj(x, g, w_stack, idx, tm):
    m = x.shape[0]
    n = w_stack.shape[2]
    return pl.pallas_call(
        _norm_proj_kernel,
        grid=(m // tm,),
        in_specs=[pl.BlockSpec((tm, D_MODEL), lambda i: (i, 0)),
                  _resident((1, D_MODEL), (0, 0)),
                  _resident((None, D_MODEL, n), (idx, 0, 0))],
        out_specs=pl.BlockSpec((tm, n), lambda i: (i, 0)),
        out_shape=jax.ShapeDtypeStruct((m, n), F32),
        compiler_params=_params("parallel"),
        name="norm_proj",
    )(x, g, w_stack)


def _out_ffn_kernel(x_ref, o_ref, wo_ref, g_ref, wg_ref, wu_ref, wd_ref, gf_ref, y_ref, *, final):
    xn = x_ref[...] + _dot(o_ref[...], wo_ref[...])
    hn = _rms(xn, g_ref[...]).astype(BF16)
    gate = _dot(hn, wg_ref[...])
    up = _dot(hn, wu_ref[...])
    y = xn + _dot(_silu(gate) * up, wd_ref[...])
    if final:
        y = _rms(y, gf_ref[...])
    y_ref[...] = y


def _out_ffn(x, o, w_out_stack, idx, g_ffn, w_up_stack, w_down_stack, layer, g_final, tm, final):
    m = x.shape[0]
    return pl.pallas_call(
        functools.partial(_out_ffn_kernel, final=final),
        grid=(m // tm,),
        in_specs=[pl.BlockSpec((tm, D_MODEL), lambda i: (i, 0)),
                  pl.BlockSpec((tm, D_MODEL), lambda i: (i, 0)),
                  _resident((None, D_MODEL, D_MODEL), (idx, 0, 0)),
                  _resident((1, D_MODEL), (0, 0)),
                  _resident((None, D_MODEL, D_FF), (layer, 0, 0)),
                  _resident((None, D_MODEL, D_FF), (layer, 0, 1)),
                  _resident((None, D_FF, D_MODEL), (layer, 0, 0)),
                  _resident((1, D_MODEL), (0, 0))],
        out_specs=pl.BlockSpec((tm, D_MODEL), lambda i: (i, 0)),
        out_shape=jax.ShapeDtypeStruct((m, D_MODEL), F32),
        compiler_params=_params("parallel"),
        name="out_ffn",
    )(x, o, w_out_stack, g_ffn, w_up_stack, w_up_stack, w_down_stack, g_final)


def _gla_logf(glr, wlr_ref, blr_ref):
    return _log_sigmoid(_dot_f32(glr, wlr_ref[...]) + blr_ref[...]) * (1.0 / GLA_TAU)


def _project_chunks(x, g_ref, w_ref, proj_ref):
    ps, c, _ = x.shape
    y = _dot(_rms(x.reshape(ps * c, D_MODEL), g_ref[...]), w_ref[...])
    proj_ref[...] = y.reshape(ps, c, D_PROJ)


def _even_prompt_kernel(xa_ref, xb_ref, g_ref, w_ref, cos_ref, sin_ref, wlr_ref, blr_ref, gng_ref,
                        gnr_ref, o_ref, sg_ref, sr_ref, p0_ref, p1_ref):
    c = EVEN_CHUNK

    @pl.when(pl.program_id(1) == 0)
    def _():
        sg_ref[...] = jnp.zeros_like(sg_ref)
        sr_ref[...] = jnp.zeros_like(sr_ref)
        _project_chunks(xa_ref[:, 0:c, :], g_ref, w_ref, p0_ref)

    def scan(p_ref, lo):
        for s in range(xa_ref.shape[0]):
            _even_chunk(p_ref.at[s], cos_ref.at[lo:lo + c], sin_ref.at[lo:lo + c], wlr_ref,
                        blr_ref, gng_ref, gnr_ref, o_ref.at[s, lo:lo + c], sg_ref.at[s],
                        sr_ref.at[s])

    scan(p0_ref, 0)
    _project_chunks(xa_ref[:, c:2 * c, :], g_ref, w_ref, p1_ref)
    scan(p1_ref, c)
    _project_chunks(xb_ref[...], g_ref, w_ref, p0_ref)


def _even_chunk(p_ref, cos_ref, sin_ref, wlr_ref, blr_ref, gng_ref, gnr_ref, o_ref, sg_ref, sr_ref):
    c = EVEN_CHUNK
    r_sub = GLA_SUB

    q = p_ref[:, 0:256] * DK_A ** -0.5
    k = p_ref[:, 256:512]
    v = p_ref[:, 512:1024]
    gg = p_ref[:, 1024:1536]
    logf = _gla_logf(p_ref[:, GATE_OFF:GATE_OFF + LANE], wlr_ref, blr_ref)
    b = _dot_f32(_tri(c), logf)
    s_gla = sg_ref[...]
    inter = _dot(_stack_heads(q * jnp.exp(b), H_A, DK_A), s_gla)
    o_a = jnp.concatenate([inter[h * c:(h + 1) * c] for h in range(H_A)], axis=1)

    ind = (_iota((H_A * DK_A, H_A * DV_A), 0) // DK_A ==
           _iota((H_A * DK_A, H_A * DV_A), 1) // DV_A).astype(BF16)
    row = _iota((r_sub, H_A * DK_A), 0)
    blocks = []
    for i in range(c // r_sub):
        lo = i * r_sub
        qb = q[lo:lo + r_sub]
        bb = b[lo:lo + r_sub]
        ws = []
        for s in range(r_sub):
            t0 = s // SUBLANE * SUBLANE
            rows_t = _iota((r_sub - t0, H_A * DK_A), 0) + t0
            e = jnp.exp(jnp.minimum(b[lo + t0:lo + r_sub] - b[lo + s:lo + s + 1], 0.0))
            ws.append(jnp.where(rows_t >= s, q[lo + t0:lo + r_sub] * k[lo + s:lo + s + 1] * e, 0.0))
        rep = _dot(jnp.concatenate(ws, axis=0), ind)
        groups = [None] * (r_sub // SUBLANE)
        off = 0
        for s in range(r_sub):
            t0 = s // SUBLANE * SUBLANE
            part = rep[off:off + r_sub - t0] * v[lo + s:lo + s + 1]
            off += r_sub - t0
            for g in range(t0 // SUBLANE, r_sub // SUBLANE):
                piece = part[g * SUBLANE - t0:(g + 1) * SUBLANE - t0]
                groups[g] = piece if groups[g] is None else groups[g] + piece
        ob = jnp.concatenate(groups, axis=0)
        if i > 0:
            ref_b = b[lo - 1:lo]
            qi = qb * jnp.exp(bb - ref_b)
            ki = k[0:lo] * jnp.exp(ref_b - b[0:lo])
            sc = _dot_nt(_stack_heads(qi, H_A, DK_A), ki)
            ob = ob + _diag_lanes(_dot(sc, v[0:lo]), H_A, r_sub, DV_A)
        blocks.append(ob)
    o_a = o_a + jnp.concatenate(blocks, axis=0)
    o_a = _head_rmsnorm(o_a, H_A, DV_A) * gng_ref[...] * _silu(gg)

    b_t = b.T
    b_last = b_t[:, c - 1:c]
    kd_t = k.T * jnp.exp(b_last - b_t)
    upd = _head_dots_rows(kd_t, v, H_A, DK_A, DV_A)
    sg_ref[...] = s_gla * jnp.exp(b_last) + upd

    cos = cos_ref[...]
    sin = sin_ref[...]
    rq = _rope(p_ref[:, 1536:1792], cos, sin, DK_B) * DK_B ** -0.5
    rk = _rope(p_ref[:, 1792:2048], cos, sin, DK_B)
    rv = p_ref[:, 2048:2560]
    rg = p_ref[:, 2560:3072]
    s_ret = sr_ref[...]
    t_row = _iota((H_B * c, c), 0)
    lg_row = _head_select(t_row // c, LOG_GAMMA)
    t_loc = (t_row % c).astype(F32)
    s_loc = _iota((H_B * c, c), 1).astype(F32)
    decay = jnp.where(t_loc >= s_loc, jnp.exp(jnp.maximum(t_loc - s_loc, 0.0) * lg_row), 0.0)
    q_dec = jnp.exp((t_loc[:, 0:1] + 1.0) * lg_row[:, 0:1])
    qs = _stack_heads(rq, H_B, DK_B)
    scores = _dot_nt(qs, rk) * decay
    o_b = jnp.concatenate([_dot(scores[h * c:(h + 1) * c], rv[:, h * DV_B:(h + 1) * DV_B])
                           for h in range(H_B)], axis=1)
    inter_b = _dot(qs * q_dec, s_ret)
    o_b = o_b + jnp.concatenate([inter_b[h * c:(h + 1) * c] for h in range(H_B)], axis=1)
    o_b = _head_groupnorm(o_b, H_B, DV_B) * gnr_ref[...] * _silu(rg)

    lg_lane = _head_select(_iota((c, H_B * DK_B), 1) // DK_B, LOG_GAMMA)
    k_dec = jnp.exp((c - 1.0 - _iota((c, H_B * DK_B), 0).astype(F32)) * lg_lane)
    upd_b = _head_dots_rows((rk * k_dec).T, rv, H_B, DK_B, DV_B)
    g_chunk = jnp.exp(c * _head_select(_iota((H_B * DK_B, DV_B), 0) // DK_B, LOG_GAMMA))
    sr_ref[...] = s_ret * g_chunk + upd_b

    o_ref[...] = jnp.concatenate([o_a, o_b], axis=1).astype(BF16)


def _even_prompt(x, g_mix, w_stack, idx, cos, sin, w_lr, b_lr, gn_gla, gn_ret, nb, seq):
    c = EVEN_CHUNK
    ps = min(EVEN_SEQS, nb)
    steps = seq // (2 * c)
    last = seq // c - 1
    small = lambda shape: pl.BlockSpec(shape, lambda b, n: (0, 0))
    per_seq = lambda rows, cols: pl.BlockSpec((ps, rows, cols), lambda b, n: (b, 0, 0))
    x = x.reshape(nb, seq, D_MODEL)
    o, sg, sr = pl.pallas_call(
        _even_prompt_kernel,
        grid=(nb // ps, steps),
        in_specs=[pl.BlockSpec((ps, 2 * c, D_MODEL), lambda b, n: (b, n, 0)),
                  pl.BlockSpec((ps, c, D_MODEL),
                               lambda b, n: (b, jnp.minimum(2 * n + 2, last), 0)),
                  small((1, D_MODEL)),
                  pl.BlockSpec((None, D_MODEL, D_PROJ), lambda b, n: (idx, 0, 0),
                               pipeline_mode=pl.Buffered(1)),
                  pl.BlockSpec((2 * c, H_B * DK_B), lambda b, n: (n, 0)),
                  pl.BlockSpec((2 * c, H_B * DK_B), lambda b, n: (n, 0)),
                  small((LANE, H_A * DK_A)), small((1, H_A * DK_A)),
                  small((1, D_HALF)), small((1, D_HALF))],
        out_specs=[pl.BlockSpec((ps, 2 * c, D_MODEL), lambda b, n: (b, n, 0)),
                   per_seq(H_A * DK_A, DV_A), per_seq(H_B * DK_B, DV_B)],
        out_shape=[jax.ShapeDtypeStruct((nb, seq, D_MODEL), BF16),
                   jax.ShapeDtypeStruct((nb, H_A * DK_A, DV_A), F32),
                   jax.ShapeDtypeStruct((nb, H_B * DK_B, DV_B), F32)],
        scratch_shapes=[pltpu.VMEM((ps, c, D_PROJ), F32), pltpu.VMEM((ps, c, D_PROJ), F32)],
        compiler_params=_params("parallel", "arbitrary"),
        name="even_prompt",
    )(x, x, g_mix, w_stack, cos, sin, w_lr, b_lr, gn_gla, gn_ret)
    return o.reshape(nb * seq, D_MODEL), sg, sr


def _rg_gates(xc, wr_ref, br_ref, wi_ref, bi_ref, lam_ref):
    r = _sigmoid(_dot(xc, wr_ref[...]) + br_ref[...])
    i = _sigmoid(_dot(xc, wi_ref[...]) + bi_ref[...])
    log_a = -RG_C * r * _softplus(-lam_ref[...])
    a = jnp.exp(log_a)
    u = jnp.sqrt(jnp.tanh(-log_a) * (a * a + 1.0)) * (i * xc)
    return a, u


def _odd_prompt_kernel(xa_ref, xb_ref, g_ref, w_ref, big_ref, bfg_ref, gnm_ref, cw_ref, cb_ref,
                       wr_ref, br_ref, wi_ref, bi_ref, lam_ref, o_ref, cst_ref, nst_ref, mst_ref,
                       hst_ref, conv_ref, xpad_ref, p0_ref, p1_ref):
    c = ODD_CHUNK

    @pl.when(pl.program_id(1) == 0)
    def _():
        cst_ref[...] = jnp.zeros_like(cst_ref)
        nst_ref[...] = jnp.zeros_like(nst_ref)
        mst_ref[...] = jnp.zeros_like(mst_ref)
        hst_ref[...] = jnp.zeros_like(hst_ref)
        xpad_ref[:, 0:8, :] = jnp.zeros((PROMPT_SEQS, 8, D_RG), F32)
        _project_chunks(xa_ref[:, 0:c, :], g_ref, w_ref, p0_ref)

    def scan(p_ref, lo):
        for s in range(PROMPT_SEQS):
            _odd_chunk(p_ref.at[s], big_ref, bfg_ref, gnm_ref, cw_ref, cb_ref, wr_ref, br_ref,
                       wi_ref, bi_ref, lam_ref, o_ref.at[s, lo:lo + c], cst_ref.at[s],
                       nst_ref.at[s], mst_ref.at[s], hst_ref.at[s], conv_ref.at[s], xpad_ref.at[s])

    scan(p0_ref, 0)
    _project_chunks(xa_ref[:, c:2 * c, :], g_ref, w_ref, p1_ref)
    scan(p1_ref, c)
    _project_chunks(xb_ref[...], g_ref, w_ref, p0_ref)


def _odd_chunk(p_ref, big_ref, bfg_ref, gnm_ref, cw_ref, cb_ref, wr_ref, br_ref, wi_ref,
               bi_ref, lam_ref, o_ref, cst_ref, nst_ref, mst_ref, hst_ref, conv_ref, xpad_ref):
    c = ODD_CHUNK

    gates = p_ref[:, GATE_OFF:GATE_OFF + LANE]
    ig = gates + big_ref[...]
    lf = _log_sigmoid(gates + bfg_ref[...])
    b_all = pltpu.roll(_cumsum_rows(lf), LANE - H_C, 1)
    a_all = ig - b_all
    a_rows = a_all.T
    causal = _iota((c, c), 0) >= _iota((c, c), 1)
    m_prev = mst_ref[...]
    m_new = m_prev
    lane = _iota((1, LANE), 1)
    outs = []
    for h in range(H_C):
        sl = slice(h * DK_C, (h + 1) * DK_C)
        qh = p_ref[:, sl] * DK_C ** -0.5
        kh = p_ref[:, 512 + h * DK_C:512 + (h + 1) * DK_C]
        vh = p_ref[:, 1024 + h * DV_C:1024 + (h + 1) * DV_C]
        bcol = b_all[:, h:h + 1]
        acol = a_all[:, h:h + 1]
        arow = a_rows[h:h + 1, :]
        mp = m_prev[:, h:h + 1]
        cmax = jnp.max(jnp.where(causal, arow, -jnp.inf), axis=1, keepdims=True)
        m_t = jnp.maximum(bcol + mp, bcol + cmax)
        w_state = jnp.exp(bcol + mp - m_t)
        w_intra = jnp.where(causal, jnp.exp(jnp.minimum(bcol + arow - m_t, 0.0)), 0.0)
        c_h = cst_ref[sl, :]
        n_h = nst_ref[h:h + 1, :]
        scores = _dot_nt(qh, kh) * w_intra
        num = w_state * _dot(qh, c_h) + _dot(scores, vh)
        den = w_state * jnp.sum(qh * n_h, axis=1, keepdims=True) + \
            jnp.sum(scores, axis=1, keepdims=True)
        outs.append(num / jnp.maximum(jnp.abs(den), jnp.exp(-m_t)))
        m_last = m_t[c - 1:c]
        b_last = bcol[c - 1:c]
        w_s_last = jnp.exp(b_last + mp - m_last)
        kw = kh * jnp.exp(b_last + acol - m_last)
        cst_ref[sl, :] = c_h * w_s_last + _dot(kw.T, vh)
        nst_ref[h:h + 1, :] = n_h * w_s_last + jnp.sum(kw, axis=0, keepdims=True)
        m_new = jnp.where(lane == h, m_last, m_new)
    mst_ref[...] = m_new
    o_c = _head_rmsnorm(jnp.concatenate(outs, axis=1), H_C, DV_C) * gnm_ref[...] * \
        _sigmoid(p_ref[:, 1536:2048])

    rx = p_ref[:, 2048:2560]
    xpad_ref[8:8 + c, :] = rx
    xc = cb_ref[...] + cw_ref[3:4, :] * rx
    for j in range(1, CONV_W):
        xc = xc + cw_ref[3 - j:4 - j, :] * xpad_ref[8 - j:8 - j + c, :]
    conv_ref[...] = xpad_ref[c + 8 - (CONV_W - 1):c + 8, :]
    xpad_ref[0:8, :] = xpad_ref[c:c + 8, :]
    a, u = _rg_gates(xc, wr_ref, br_ref, wi_ref, bi_ref, lam_ref)
    a = a.reshape(c // SUBLANE, SUBLANE, D_RG)
    u = u.reshape(c // SUBLANE, SUBLANE, D_RG)
    t_idx = _iota(a.shape, 1)
    s = 1
    while s < SUBLANE:
        keep = t_idx >= s
        u = jnp.where(keep, a * pltpu.roll(u, s, 1) + u, u)
        a = jnp.where(keep, a * pltpu.roll(a, s, 1), a)
        s *= 2
    carry = hst_ref[...]
    groups = []
    for g in range(c // SUBLANE):
        hg = a[g] * carry + u[g]
        carry = hg[SUBLANE - 1:SUBLANE]
        groups.append(hg)
    hs = jnp.concatenate(groups, axis=0)
    hst_ref[...] = carry
    o_d = hs * _gelu_tanh(p_ref[:, 2560:3072])

    o_ref[...] = jnp.concatenate([o_c, o_d], axis=1).astype(BF16)


def _odd_prompt(x, g_mix, w_stack, idx, b_ig, b_fg, gn_ml, conv_w, conv_b, w_r, b_r, w_i, b_i, lam,
                nb, seq):
    c = ODD_CHUNK
    ps = PROMPT_SEQS
    steps = seq // (2 * c)
    last = seq // c - 1
    small = lambda shape: pl.BlockSpec(shape, lambda b, n: (0, 0))
    per_seq = lambda rows, cols: pl.BlockSpec((ps, rows, cols), lambda b, n: (b, 0, 0))
    x = x.reshape(nb, seq, D_MODEL)
    outs = pl.pallas_call(
        _odd_prompt_kernel,
        grid=(nb // ps, steps),
        in_specs=[pl.BlockSpec((ps, 2 * c, D_MODEL), lambda b, n: (b, n, 0)),
                  pl.BlockSpec((ps, c, D_MODEL),
                               lambda b, n: (b, jnp.minimum(2 * n + 2, last), 0)),
                  small((1, D_MODEL)),
                  pl.BlockSpec((None, D_MODEL, D_PROJ), lambda b, n: (idx, 0, 0),
                               pipeline_mode=pl.Buffered(1)),
                  small((1, LANE)), small((1, LANE)), small((1, D_HALF)),
                  small((CONV_W, D_RG)), small((1, D_RG)),
                  small((D_RG, D_RG)), small((1, D_RG)), small((D_RG, D_RG)), small((1, D_RG)),
                  small((1, D_RG))],
        out_specs=[pl.BlockSpec((ps, 2 * c, D_MODEL), lambda b, n: (b, n, 0)),
                   per_seq(H_C * DK_C, DV_C), per_seq(H_C, DK_C), per_seq(1, LANE),
                   per_seq(1, D_RG), per_seq(CONV_W - 1, D_RG)],
        out_shape=[jax.ShapeDtypeStruct((nb, seq, D_MODEL), BF16),
                   jax.ShapeDtypeStruct((nb, H_C * DK_C, DV_C), F32),
                   jax.ShapeDtypeStruct((nb, H_C, DK_C), F32),
                   jax.ShapeDtypeStruct((nb, 1, LANE), F32),
                   jax.ShapeDtypeStruct((nb, 1, D_RG), F32),
                   jax.ShapeDtypeStruct((nb, CONV_W - 1, D_RG), F32)],
        scratch_shapes=[pltpu.VMEM((ps, c + 8, D_RG), F32),
                        pltpu.VMEM((ps, c, D_PROJ), F32), pltpu.VMEM((ps, c, D_PROJ), F32)],
        compiler_params=_params("parallel", "arbitrary"),
        name="odd_prompt",
    )(x, x, g_mix, w_stack, b_ig, b_fg, gn_ml, conv_w, conv_b, w_r, b_r, w_i, b_i, lam)
    return (outs[0].reshape(nb * seq, D_MODEL),) + tuple(outs[1:])


def _rank1_step(s_in_ref, s_out_ref, o_scr_ref, q_t, k_t, d_t, v, n_heads, dk, dv):
    bb = v.shape[0]
    for j in range(bb):
        vst = jnp.concatenate(
            [jnp.broadcast_to(v[j:j + 1, h * dv:(h + 1) * dv], (dk, dv)) for h in range(n_heads)],
            axis=0)
        s_new = s_in_ref[j] * d_t[:, j:j + 1] + k_t[:, j:j + 1] * vst
        s_out_ref[j] = s_new
        prod = q_t[:, j:j + 1] * s_new
        o_scr_ref[j:j + 1, :] = jnp.concatenate(
            [jnp.sum(prod[h * dk:(h + 1) * dk], axis=0, keepdims=True) for h in range(n_heads)],
            axis=1)


def _even_step_kernel(*refs, n_prev):
    (p_ref, cos_ref, sin_ref, wlr_ref, blr_ref, gng_ref, gnr_ref, sg_in_ref, sr_in_ref) = refs[:9]
    o_ref, sg_out_ref, sr_out_ref, oa_ref, ob_ref = refs[9 + n_prev:]
    bb = STEP_BB
    q = p_ref[:, 0:256] * DK_A ** -0.5
    k = p_ref[:, 256:512]
    v = p_ref[:, 512:1024]
    gg = p_ref[:, 1024:1536]
    d = jnp.exp(_gla_logf(p_ref[:, GATE_OFF:GATE_OFF + LANE], wlr_ref, blr_ref))
    _rank1_step(sg_in_ref, sg_out_ref, oa_ref, q.T, k.T, d.T, v, H_A, DK_A, DV_A)
    o_a = _head_rmsnorm(oa_ref[...], H_A, DV_A) * gng_ref[...] * _silu(gg)

    cos = cos_ref[...]
    sin = sin_ref[...]
    rq = _rope(p_ref[:, 1536:1792], cos, sin, DK_B) * DK_B ** -0.5
    rk = _rope(p_ref[:, 1792:2048], cos, sin, DK_B)
    rv = p_ref[:, 2048:2560]
    rg = p_ref[:, 2560:3072]
    gamma_t = jnp.exp(_head_select(_iota((H_B * DK_B, bb), 0) // DK_B, LOG_GAMMA))
    _rank1_step(sr_in_ref, sr_out_ref, ob_ref, rq.T, rk.T, gamma_t, rv, H_B, DK_B, DV_B)
    o_b = _head_groupnorm(ob_ref[...], H_B, DV_B) * gnr_ref[...] * _silu(rg)
    o_ref[...] = jnp.concatenate([o_a, o_b], axis=1).astype(BF16)


def _even_step(proj, cos, sin, w_lr, b_lr, gn_gla, gn_ret, s_gla, s_ret, idx, prev):
    nb = proj.shape[0]
    bb = STEP_BB
    small = lambda shape: pl.BlockSpec(shape, lambda i: (0, 0))
    state = lambda rows, cols: pl.BlockSpec((None, bb, rows, cols), lambda i: (idx, i, 0, 0))
    n_in = 9
    return pl.pallas_call(
        functools.partial(_even_step_kernel, n_prev=len(prev)),
        grid=(nb // bb,),
        in_specs=[pl.BlockSpec((bb, D_PROJ), lambda i: (i, 0)),
                  small((1, H_B * DK_B)), small((1, H_B * DK_B)),
                  small((LANE, H_A * DK_A)), small((1, H_A * DK_A)),
                  small((1, D_HALF)), small((1, D_HALF)),
                  state(H_A * DK_A, DV_A), state(H_B * DK_B, DV_B)] +
                 [pl.BlockSpec(memory_space=pl.ANY)] * len(prev),
        out_specs=[pl.BlockSpec((bb, D_MODEL), lambda i: (i, 0)),
                   state(H_A * DK_A, DV_A), state(H_B * DK_B, DV_B)],
        out_shape=[jax.ShapeDtypeStruct((nb, D_MODEL), BF16),
                   jax.ShapeDtypeStruct(s_gla.shape, F32),
                   jax.ShapeDtypeStruct(s_ret.shape, F32)],
        input_output_aliases={n_in + j: 1 + j for j in range(len(prev))},
        scratch_shapes=[pltpu.VMEM((bb, D_HALF), F32), pltpu.VMEM((bb, D_HALF), F32)],
        compiler_params=_params("parallel"),
        name="even_step",
    )(proj, cos, sin, w_lr, b_lr, gn_gla, gn_ret, s_gla, s_ret, *prev)


def _odd_step_kernel(*refs, n_prev):
    (p_ref, big_ref, bfg_ref, gnm_ref, cw_ref, cb_ref, wr_ref, br_ref, wi_ref, bi_ref, lam_ref,
     c_in_ref, n_in_ref, m_in_ref, h_in_ref, conv_in_ref) = refs[:16]
    (o_ref, c_out_ref, n_out_ref, m_out_ref, h_out_ref, conv_out_ref, num_ref) = refs[16 + n_prev:]
    bb = STEP_BB
    gates = p_ref[:, GATE_OFF:GATE_OFF + LANE]
    ig = gates + big_ref[...]
    lf = pltpu.roll(_log_sigmoid(gates + bfg_ref[...]), LANE - H_C, 1)
    m_prev = m_in_ref[...]
    m_new = jnp.maximum(lf + m_prev, ig)
    valid = _iota((bb, LANE), 1) < H_C
    w_s = jnp.where(valid, jnp.exp(lf + m_prev - m_new), 0.0)
    w_i = jnp.where(valid, jnp.exp(ig - m_new), 0.0)
    m_out_ref[...] = jnp.where(valid, m_new, 0.0)
    expand = (_iota((LANE, H_C * DK_C), 0) == _iota((LANE, H_C * DK_C), 1) // DK_C).astype(F32)
    w_s_wide = _dot_f32(w_s, expand)
    w_i_wide = _dot_f32(w_i, expand)
    q = p_ref[:, 0:512] * DK_C ** -0.5
    kw = p_ref[:, 512:1024] * w_i_wide
    v = p_ref[:, 1024:1536]
    _rank1_step(c_in_ref, c_out_ref, num_ref, q.T, kw.T, w_s_wide.T, v, H_C, DK_C, DV_C)
    n_new = n_in_ref[...] * w_s_wide + kw
    n_out_ref[...] = n_new
    num = num_ref[...]
    qn = q * n_new
    outs = []
    for h in range(H_C):
        sl = slice(h * DK_C, (h + 1) * DK_C)
        den = jnp.sum(qn[:, sl], axis=1, keepdims=True)
        outs.append(num[:, sl] / jnp.maximum(jnp.abs(den), jnp.exp(-m_new[:, h:h + 1])))
    o_c = _head_rmsnorm(jnp.concatenate(outs, axis=1), H_C, DV_C) * gnm_ref[...] * \
        _sigmoid(p_ref[:, 1536:2048])

    rx = p_ref[:, 2048:2560]
    xc = cb_ref[...] + cw_ref[3:4, :] * rx
    for j in range(CONV_W - 1):
        xc = xc + cw_ref[j:j + 1, :] * conv_in_ref[j]
    for j in range(CONV_W - 2):
        conv_out_ref[j] = conv_in_ref[j + 1]
    conv_out_ref[CONV_W - 2] = rx
    a, u = _rg_gates(xc, wr_ref, br_ref, wi_ref, bi_ref, lam_ref)
    hs = a * h_in_ref[...] + u
    h_out_ref[...] = hs
    o_d = hs * _gelu_tanh(p_ref[:, 2560:3072])
    o_ref[...] = jnp.concatenate([o_c, o_d], axis=1).astype(BF16)


def _odd_step(proj, b_ig, b_fg, gn_ml, conv_w, conv_b, w_r, b_r, w_i, b_i, lam,
              s_c, s_n, s_m, s_h, s_conv, idx, prev):
    nb = proj.shape[0]
    bb = STEP_BB
    small = lambda shape: pl.BlockSpec(shape, lambda i: (0, 0))
    rows = lambda cols: pl.BlockSpec((bb, cols), lambda i: (i, 0))
    c_spec = pl.BlockSpec((None, bb, H_C * DK_C, DV_C), lambda i: (idx, i, 0, 0))
    conv_spec = pl.BlockSpec((CONV_W - 1, bb, D_RG), lambda i: (0, i, 0))
    n_in = 16
    return pl.pallas_call(
        functools.partial(_odd_step_kernel, n_prev=len(prev)),
        grid=(nb // bb,),
        in_specs=[rows(D_PROJ),
                  small((1, LANE)), small((1, LANE)), small((1, D_HALF)),
                  small((CONV_W, D_RG)), small((1, D_RG)),
                  small((D_RG, D_RG)), small((1, D_RG)), small((D_RG, D_RG)), small((1, D_RG)),
                  small((1, D_RG)),
                  c_spec, rows(H_C * DK_C), rows(LANE), rows(D_RG), conv_spec] +
                 [pl.BlockSpec(memory_space=pl.ANY)] * len(prev),
        out_specs=[rows(D_MODEL), c_spec, rows(H_C * DK_C), rows(LANE), rows(D_RG), conv_spec],
        out_shape=[jax.ShapeDtypeStruct((nb, D_MODEL), BF16),
                   jax.ShapeDtypeStruct(s_c.shape, F32),
                   jax.ShapeDtypeStruct(s_n.shape, F32),
                   jax.ShapeDtypeStruct(s_m.shape, F32),
                   jax.ShapeDtypeStruct(s_h.shape, F32),
                   jax.ShapeDtypeStruct(s_conv.shape, F32)],
        input_output_aliases={n_in + j: 1 + j for j in range(len(prev))},
        scratch_shapes=[pltpu.VMEM((bb, D_HALF), F32)],
        compiler_params=_params("parallel"),
        name="odd_step",
    )(proj, b_ig, b_fg, gn_ml, conv_w, conv_b, w_r, b_r, w_i, b_i, lam, s_c, s_n, s_m, s_h, s_conv,
      *prev)


def _rope_tables(pos):
    half = DK_B // 2
    inv = ROPE_BASE ** (-jnp.arange(half, dtype=F32) / half)
    ang = pos.astype(F32)[:, None] * inv[None, :]
    cos = jnp.tile(jnp.cos(ang), (1, 2 * H_B))
    sin = jnp.tile(jnp.concatenate([-jnp.sin(ang), jnp.sin(ang)], axis=1), (1, H_B))
    return cos, sin


PREP_ROWS = 256


def _reorder_cols_kernel(w_ref, o_ref, *, gate_lo, gate_hi):
    w = w_ref[...]
    pad = jnp.zeros((w.shape[0], LANE - (gate_hi - gate_lo)), F32)
    o_ref[:, 0:gate_lo] = w[:, 0:gate_lo].astype(BF16)
    o_ref[:, gate_lo:GATE_OFF] = w[:, gate_hi:].astype(BF16)
    o_ref[:, GATE_OFF:] = jnp.concatenate([w[:, gate_lo:gate_hi], pad], axis=1).astype(BF16)


def _reorder_cols(w, gate_lo, gate_hi):
    layers, rows, cols = w.shape
    return pl.pallas_call(
        functools.partial(_reorder_cols_kernel, gate_lo=gate_lo, gate_hi=gate_hi),
        grid=(layers, rows // PREP_ROWS),
        in_specs=[pl.BlockSpec((None, PREP_ROWS, cols), lambda l, r: (l, r, 0))],
        out_specs=pl.BlockSpec((None, PREP_ROWS, D_PROJ), lambda l, r: (l, r, 0)),
        out_shape=jax.ShapeDtypeStruct((layers, rows, D_PROJ), BF16),
        compiler_params=_params("parallel", "parallel"),
        name="reorder_cols",
    )(w)


def _pad_lanes(x, offset=0):
    return jnp.pad(x.reshape(1, -1), ((0, 0), (offset, LANE - offset - x.size)))


def _block_diag(w):
    return jax.scipy.linalg.block_diag(*[w[i] for i in range(RG_BLOCKS)]).astype(BF16)


def kernel(x_prompt, x_sample, state_gla, state_ret, state_mlstm_C, state_mlstm_n, state_mlstm_m, state_rglru_h, state_rglru_conv, norm_mix, norm_ffn, norm_final, even_w_in, even_w_lr, even_b_lr, even_gn_gla, even_gn_ret, even_w_out, odd_w_in, ml_b_i, ml_b_f, odd_gn_ml, rg_conv_w, rg_conv_b, rg_w_r, rg_b_r, rg_w_i, rg_b_i, rg_lam, odd_w_out, ffn_w_up, ffn_w_down):
    nb, seq, _ = x_prompt.shape
    ns = x_sample.shape[0]
    row = lambda x: x.reshape(1, -1)

    even_w_in_b = _reorder_cols(even_w_in, 1536, 1536 + GLA_RANK)
    odd_w_in_b = _reorder_cols(odd_w_in, 1536, 1536 + 2 * H_C)
    even_w_out_b = even_w_out.astype(BF16)
    odd_w_out_b = odd_w_out.astype(BF16)
    w_up_b = ffn_w_up.astype(BF16)
    w_down_b = ffn_w_down.astype(BF16)
    s_gla = state_gla.reshape(N_EVEN, ns, H_A * DK_A, DV_A)
    s_ret = state_ret.reshape(N_EVEN, ns, H_B * DK_B, DV_B)
    s_c = state_mlstm_C.reshape(N_ODD, ns, H_C * DK_C, DV_C)
    cos_p, sin_p = _rope_tables(jnp.arange(seq, dtype=jnp.int32))
    cos_s, sin_s = _rope_tables(PAST_LEN + jnp.arange(1, dtype=jnp.int32))

    xp = x_prompt.reshape(nb * seq, D_MODEL)
    xs = x_sample.reshape(ns, D_MODEL)
    st = {k: [] for k in ("gla_p", "ret_p", "c_p", "n_p", "m_p", "h_p", "conv_p",
                          "n_s", "m_s", "h_s", "conv_s")}
    even_new, c_new = [], []
    for layer in range(DEPTH):
        g_mix = row(norm_mix[layer])
        final = layer == DEPTH - 1
        if layer % 2 == 0:
            e = layer // 2
            w_in, w_out, idx = even_w_in_b, even_w_out_b, e
            w_lr = jnp.pad(even_w_lr[e], ((0, LANE - GLA_RANK), (0, 0)))
            small = (w_lr, row(even_b_lr[e]), row(even_gn_gla[e]), row(even_gn_ret[e]))
            o_p, sg, sr = _even_prompt(xp, g_mix, w_in, idx, cos_p, sin_p, *small, nb, seq)
            st["gla_p"].append(sg.reshape(nb, H_A, DK_A, DV_A))
            st["ret_p"].append(sr.reshape(nb, H_B, DK_B, DV_B))
            o_s, *even_new = _even_step(_norm_proj(xs, g_mix, w_in, idx, ns), cos_s, sin_s,
                                        *small, s_gla, s_ret, e, even_new)
        else:
            o = layer // 2
            w_in, w_out, idx = odd_w_in_b, odd_w_out_b, o
            small = (_pad_lanes(ml_b_i[o]), _pad_lanes(ml_b_f[o], H_C), row(odd_gn_ml[o]),
                     rg_conv_w[o], row(rg_conv_b[o]), _block_diag(rg_w_r[o]), row(rg_b_r[o]),
                     _block_diag(rg_w_i[o]), row(rg_b_i[o]), row(rg_lam[o]))
            o_p, c_, n_, m_, h_, cv = _odd_prompt(xp, g_mix, w_in, idx, *small, nb, seq)
            st["c_p"].append(c_.reshape(nb, H_C, DK_C, DV_C))
            st["n_p"].append(n_)
            st["m_p"].append(m_[:, 0, :H_C])
            st["h_p"].append(h_[:, 0, :])
            st["conv_p"].append(cv)
            o_s, c_, n_, m_, h_, cv = _odd_step(
                _norm_proj(xs, g_mix, w_in, idx, ns), *small,
                s_c,
                state_mlstm_n[o].reshape(ns, H_C * DK_C),
                jnp.pad(state_mlstm_m[o], ((0, 0), (0, LANE - H_C))),
                state_rglru_h[o],
                jnp.swapaxes(state_rglru_conv[o], 0, 1), o, c_new)
            c_new = [c_]
            st["n_s"].append(n_.reshape(ns, H_C, DK_C))
            st["m_s"].append(m_[:, :H_C])
            st["h_s"].append(h_)
            st["conv_s"].append(jnp.swapaxes(cv, 0, 1))
        ffn = (row(norm_ffn[layer]), w_up_b, w_down_b, layer, row(norm_final))
        xp = _out_ffn(xp, o_p, w_out, idx, *ffn, 512, final)
        xs = _out_ffn(xs, o_s, w_out, idx, *ffn, ns, final)

    stack = lambda name: jnp.stack(st[name])
    return (xp.reshape(nb, seq, D_MODEL), xs.reshape(ns, 1, D_MODEL),
            stack("gla_p"), stack("ret_p"), stack("c_p"), stack("n_p"), stack("m_p"),
            stack("h_p"), stack("conv_p"),
            even_new[0].reshape(state_gla.shape), even_new[1].reshape(state_ret.shape),
            c_new[0].reshape(state_mlstm_C.shape), stack("n_s"), stack("m_s"),
            stack("h_s"), stack("conv_s"))
```
